```python
import jax
import jax.numpy as jnp
from jax import lax

D_MODEL = 1024
BATCH = 32
SEQ = 256
DEPTH = 4
DEC_BATCH = 8
DEC_SEQ = 4096
PAST_LEN = 512

GRID_W = 64
HEAD_DIM = 64
A_HEADS = 8
A_KV_HEADS = 2
A_GROUPS = A_HEADS // A_KV_HEADS
A_WIDTH = A_HEADS * HEAD_DIM
B_HEADS = 8
B_NOPE = 64
B_ROPE = 32
B_VDIM = 64
B_KV_RANK = 128
B_WIDTH = B_HEADS * B_VDIM
C_HEADS = 8
C_WIDTH = C_HEADS * HEAD_DIM
C_DECAY_LORA = 64
C_AAA_LORA = 64
C_SHIFT_DIM = 3 * C_WIDTH + 2 * C_DECAY_LORA + 2 * C_AAA_LORA
N_BRANCHES = 3
IN_SPLITS = (A_WIDTH, A_KV_HEADS * HEAD_DIM, A_KV_HEADS * HEAD_DIM, A_WIDTH,
             B_HEADS * (B_NOPE + B_ROPE), B_KV_RANK, B_ROPE, B_WIDTH,
             C_SHIFT_DIM, C_WIDTH, N_BRANCHES * D_MODEL)
IN_DIM = sum(IN_SPLITS)
Q_BLOCK = 128
ROPE_THETA = 10000.0
NORM_EPS = 1e-6
C_GN_EPS = 64e-5

kernel_name = 'bidir_hybrid_flow_trunk_step'


def _rmsnorm(x, w):
    xf = x.astype(jnp.float32)
    y = xf * lax.rsqrt(jnp.mean(xf * xf, axis=-1, keepdims=True) + NORM_EPS)
    return (y * w.astype(jnp.float32)).astype(x.dtype)


def _split_in(u):
    idx = []
    acc = 0
    for s in IN_SPLITS[:-1]:
        acc += s
        idx.append(acc)
    return jnp.split(u, idx, axis=-1)


def _modulation(cond, w, b):
    m = jax.nn.silu(cond) @ w + b
    return jnp.split(m, 3, axis=-1)


def _grid_positions(n_tokens):
    rows = n_tokens // GRID_W
    row = jnp.repeat(jnp.arange(rows, dtype=jnp.int32), GRID_W)
    col = jnp.tile(jnp.arange(GRID_W, dtype=jnp.int32), rows)
    return row, col


def _rope_1d(x, pos):
    half = x.shape[-1] // 2
    inv = ROPE_THETA ** (-jnp.arange(half, dtype=jnp.float32) / half)
    ang = pos.astype(jnp.float32)[:, None] * inv[None, :]
    cos = jnp.cos(ang)[:, None, :]
    sin = jnp.sin(ang)[:, None, :]
    xf = x.astype(jnp.float32)
    x1, x2 = xf[..., :half], xf[..., half:]
    out = jnp.concatenate([x1 * cos - x2 * sin, x2 * cos + x1 * sin], axis=-1)
    return out.astype(x.dtype)


def _rope_2d(x, row, col):
    half = x.shape[-1] // 2
    return jnp.concatenate([_rope_1d(x[..., :half], row), _rope_1d(x[..., half:], col)], axis=-1)


def _block_attention(q, k, v):
    bsz, sq = q.shape[0], q.shape[1]
    n_blocks = sq // Q_BLOCK
    scale = q.shape[-1] ** -0.5
    qb = jnp.swapaxes(q.reshape((bsz, n_blocks, Q_BLOCK) + q.shape[2:]), 0, 1)

    def one_block(q_blk):
        s = jnp.einsum('bqkgd,bskd->bkgqs', q_blk, k, preferred_element_type=jnp.float32) * scale
        p = jax.nn.softmax(s, axis=-1)
        return jnp.einsum('bkgqs,bskd->bqkgd', p.astype(v.dtype), v)

    o = lax.map(one_block, qb)
    return jnp.swapaxes(o, 0, 1).reshape(bsz, sq, -1)


def _shift_centered(s, mu_prev, mu_next):
    prev = jnp.pad(s[:, :-1], ((0, 0), (1, 0), (0, 0)))
    nxt = jnp.pad(s[:, 1:], ((0, 0), (0, 1), (0, 0)))
    return s + mu_prev * (prev - s) + mu_next * (nxt - s)


def _wkv_scan(s0, r, w, k, v, a_vec, b_vec, reverse):
    xs = tuple(jnp.moveaxis(t, 1, 0) for t in (r, w, k, v, a_vec, b_vec))

    def step(state, inp):
        r_t, w_t, k_t, v_t, a_t, b_t = inp
        sa = jnp.einsum('bhij,bhj->bhi', state, a_t)
        state = (state * w_t[:, :, None, :] + sa[..., None] * b_t[:, :, None, :]
                 + v_t[..., None] * k_t[:, :, None, :])
        y_t = jnp.einsum('bhij,bhj->bhi', state, r_t)
        return state, y_t

    s_fin, ys = lax.scan(step, s0, xs, reverse=reverse)
    return s_fin, jnp.moveaxis(ys, 0, 1)


def _branch_c(c_in, s0_f, s0_b, lp):
    f32 = jnp.float32
    bsz, t = c_in.shape[:2]
    x = _shift_centered(c_in, lp['c_mu_prev'], lp['c_mu_next']).astype(f32)
    wdt = C_WIDTH
    r = x[..., :wdt]
    k = x[..., wdt:2 * wdt]
    v = x[..., 2 * wdt:3 * wdt]
    wd = x[..., 3 * wdt:3 * wdt + 2 * C_DECAY_LORA].reshape(bsz, t, 2, C_DECAY_LORA)
    ad = x[..., 3 * wdt + 2 * C_DECAY_LORA:].reshape(bsz, t, 2, C_AAA_LORA)

    def hd(z):
        return z.reshape(bsz, t, C_HEADS, HEAD_DIM)

    w0 = lp['c_w0'].astype(f32)
    w_up = lp['c_w_up'].astype(f32)
    a0 = lp['c_a0'].astype(f32)
    a_up = lp['c_a_up'].astype(f32)
    k_a = lp['c_k_a'].astype(f32)
    r_k = lp['c_r_k'].astype(f32)
    rh, vh = hd(r), hd(v)
    kk = hd(k * lp['c_k_k'].astype(f32))
    kk = kk / jnp.maximum(jnp.sqrt(jnp.sum(kk * kk, axis=-1, keepdims=True)), 1e-12)
    ys, bonuses, finals = [], [], []
    for d, s0 in ((0, s0_f), (1, s0_b)):
        w_raw = w0[d] + jnp.tanh(wd[:, :, d]) @ w_up[d]
        decay = jnp.exp(-jnp.exp(-jax.nn.softplus(-w_raw) - 0.5))
        a = jax.nn.sigmoid(a0[d] + ad[:, :, d] @ a_up[d])
        k_d = hd(k * (1.0 + (a - 1.0) * k_a))
        s_fin, y_d = _wkv_scan(s0.astype(f32), rh, hd(decay), k_d, vh, -kk, kk * hd(a), reverse=(d == 1))
        ys.append(y_d)
        bonuses.append(jnp.sum(rh * k_d * r_k, axis=-1, keepdims=True) * vh)
        finals.append(s_fin)
    y = ys[0] + ys[1]
    mu = jnp.mean(y, axis=-1, keepdims=True)
    var = jnp.mean(jnp.square(y - mu), axis=-1, keepdims=True)
    yn = ((y - mu) * lax.rsqrt(var + C_GN_EPS)).reshape(bsz, t, wdt)
    yn = yn * lp['c_lnx_w'].astype(f32) + lp['c_lnx_b'].astype(f32)
    yn = yn + (bonuses[0] + bonuses[1]).reshape(bsz, t, wdt)
    return yn, finals[0], finals[1]


def _a_qkv(a_q, a_k, a_v, lp):
    bsz, t = a_q.shape[:2]
    q = _rmsnorm(a_q.reshape(bsz, t, A_HEADS, HEAD_DIM), lp['a_qnorm_w'])
    k = _rmsnorm(a_k.reshape(bsz, t, A_KV_HEADS, HEAD_DIM), lp['a_knorm_w'])
    v = a_v.reshape(bsz, t, A_KV_HEADS, HEAD_DIM)
    return q, k, v


def _mla_attend(q, ckv_all, kr_all, lp):
    bsz, s = ckv_all.shape[:2]
    k_nope = (ckv_all @ lp['b_w_uk']).reshape(bsz, s, B_HEADS, B_NOPE)
    k_rope = jnp.broadcast_to(kr_all[:, :, None, :], (bsz, s, B_HEADS, B_ROPE))
    k = jnp.concatenate([k_nope, k_rope], axis=-1)
    v = (ckv_all @ lp['b_w_uv']).reshape(bsz, s, B_HEADS, B_VDIM)
    return _block_attention(q[:, :, :, None, :], k, v)


def _merge(gates, ya, yb, yc, lp):
    ga, gb, gc = jnp.split(jax.nn.sigmoid(gates), 3, axis=-1)
    mixed = ga * (ya @ lp['w_oa']) + gb * (yb @ lp['w_ob']) + gc * (yc @ lp['w_oc'])
    return mixed @ lp['w_out']


def _context_mixer(h, lp):
    bsz, t = h.shape[:2]
    a_q, a_k, a_v, a_z, b_q, b_ckv, b_kr, b_z, c_in, c_z, gates = _split_in(h @ lp['w_in'])
    q, k, v = _a_qkv(a_q, a_k, a_v, lp)
    ya = _block_attention(q.reshape(bsz, t, A_KV_HEADS, A_GROUPS, HEAD_DIM), k, v) * jax.nn.silu(a_z)
    qb = b_q.reshape(bsz, t, B_HEADS, B_NOPE + B_ROPE)
    ckv = _rmsnorm(b_ckv, lp['b_kvnorm_w'])
    yb = _mla_attend(qb, ckv, b_kr, lp) * jax.nn.silu(b_z)
    s0 = jnp.zeros((bsz, C_HEADS, HEAD_DIM, HEAD_DIM), jnp.float32)
    yc, s_f, s_b = _branch_c(c_in, s0, s0, lp)
    yc = yc.astype(h.dtype) * jax.nn.silu(c_z)
    return _merge(gates, ya, yb, yc, lp), k, v, ckv, b_kr, s_f, s_b


def _latent_mixer(h, lp, row, col, ctx_k, ctx_v, ctx_ckv, ctx_kr, s0_f, s0_b):
    bsz, t = h.shape[:2]
    a_q, a_k, a_v, a_z, b_q, b_ckv, b_kr, b_z, c_in, c_z, gates = _split_in(h @ lp['w_in'])
    q, k, v = _a_qkv(a_q, a_k, a_v, lp)
    q = _rope_2d(q, row, col)
    k = _rope_2d(k, row, col)
    k_all = jnp.concatenate([k, ctx_k.astype(k.dtype)], axis=1)
    v_all = jnp.concatenate([v, ctx_v.astype(v.dtype)], axis=1)
    ya = _block_attention(q.reshape(bsz, t, A_KV_HEADS, A_GROUPS, HEAD_DIM), k_all, v_all) * jax.nn.silu(a_z)
    qb = b_q.reshape(bsz, t, B_HEADS, B_NOPE + B_ROPE)
    qb = jnp.concatenate([qb[..., :B_NOPE], _rope_2d(qb[..., B_NOPE:], row, col)], axis=-1)
    kr = _rope_2d(b_kr[:, :, None, :], row, col)[:, :, 0, :]
    ckv = _rmsnorm(b_ckv, lp['b_kvnorm_w'])
    ckv_all = jnp.concatenate([ckv, ctx_ckv.astype(ckv.dtype)], axis=1)
    kr_all = jnp.concatenate([kr, ctx_kr.astype(kr.dtype)], axis=1)
    yb = _mla_attend(qb, ckv_all, kr_all, lp) * jax.nn.silu(b_z)
    yc, _, _ = _branch_c(c_in, s0_f, s0_b, lp)
    yc = yc.astype(h.dtype) * jax.nn.silu(c_z)
    return _merge(gates, ya, yb, yc, lp)


def setup_inputs(seed: int = 0) -> dict:
    key = jax.random.key(seed)
    keys = list(jax.random.split(key, 48))

    def nrm(shape, scale):
        return jax.random.normal(keys.pop(), shape, jnp.float32) * scale

    def uni(shape, lo, hi):
        return jax.random.uniform(keys.pop(), shape, jnp.float32, lo, hi)

    D = D_MODEL
    return {
        'x_prompt': nrm((BATCH, SEQ, D), 1.0),
        'x_sample': nrm((DEC_BATCH, DEC_SEQ, D), 1.0),
        'cache_a_k': nrm((DEC_BATCH, DEPTH, PAST_LEN, A_KV_HEADS, HEAD_DIM), 1.0),
        'cache_a_v': nrm((DEC_BATCH, DEPTH, PAST_LEN, A_KV_HEADS, HEAD_DIM), 1.0),
        'cache_b_ckv': nrm((DEC_BATCH, DEPTH, PAST_LEN, B_KV_RANK), 1.0),
        'cache_b_krope': nrm((DEC_BATCH, DEPTH, PAST_LEN, B_ROPE), 1.0),
        'state_c_fwd': nrm((DEC_BATCH, DEPTH, C_HEADS, HEAD_DIM, HEAD_DIM), 1.0),
        'state_c_bwd': nrm((DEC_BATCH, DEPTH, C_HEADS, HEAD_DIM, HEAD_DIM), 1.0),
        'c': nrm((DEC_BATCH, D), 1.0),
        'c_ctx': nrm((D,), 1.0),
        'norm_w': 1.0 + nrm((DEPTH, D), 0.1),
        'w_mod': nrm((DEPTH, D, 3 * D), 0.5 * D ** -0.5),
        'b_mod': nrm((DEPTH, 3 * D), 0.02),
        'w_in': nrm((DEPTH, D, IN_DIM), D ** -0.5),
        'a_qnorm_w': 1.0 + nrm((DEPTH, HEAD_DIM), 0.1),
        'a_knorm_w': 1.0 + nrm((DEPTH, HEAD_DIM), 0.1),
        'b_kvnorm_w': 1.0 + nrm((DEPTH, B_KV_RANK), 0.1),
        'b_w_uk': nrm((DEPTH, B_KV_RANK, B_HEADS * B_NOPE), B_KV_RANK ** -0.5),
        'b_w_uv': nrm((DEPTH, B_KV_RANK, B_HEADS * B_VDIM), B_KV_RANK ** -0.5),
        'c_mu_prev': uni((DEPTH, C_SHIFT_DIM), 0.0, 0.5),
        'c_mu_next': uni((DEPTH, C_SHIFT_DIM), 0.0, 0.5),
        'c_w0': uni((DEPTH, 2, C_WIDTH), -4.0, 1.0),
        'c_w_up': nrm((DEPTH, 2, C_DECAY_LORA, C_WIDTH), 0.5 * C_DECAY_LORA ** -0.5),
        'c_a0': nrm((DEPTH, 2, C_WIDTH), 0.1),
        'c_a_up': nrm((DEPTH, 2, C_AAA_LORA, C_WIDTH), 0.5 * C_AAA_LORA ** -0.5),
        'c_k_k': 0.85 + nrm((DEPTH, C_WIDTH), 0.05),
        'c_k_a': 1.0 + nrm((DEPTH, C_WIDTH), 0.05),
        'c_r_k': nrm((DEPTH, C_HEADS, HEAD_DIM), 0.1),
        'c_lnx_w': 1.0 + nrm((DEPTH, C_WIDTH), 0.1),
        'c_lnx_b': nrm((DEPTH, C_WIDTH), 0.01),
        'w_oa': nrm((DEPTH, A_WIDTH, D), A_WIDTH ** -0.5),
        'w_ob': nrm((DEPTH, B_WIDTH, D), B_WIDTH ** -0.5),
        'w_oc': nrm((DEPTH, C_WIDTH, D), C_WIDTH ** -0.5),
        'w_out': nrm((DEPTH, D, D), D ** -0.5),
        'final_norm_w': 1.0 + nrm((D,), 0.1),
    }


def reference(x_prompt, x_sample, cache_a_k, cache_a_v, cache_b_ckv, cache_b_krope, state_c_fwd, state_c_bwd,
              c, c_ctx, norm_w, w_mod, b_mod, w_in, a_qnorm_w, a_knorm_w, b_kvnorm_w, b_w_uk, b_w_uv,
              c_mu_prev, c_mu_next, c_w0, c_w_up, c_a0, c_a_up, c_k_k, c_k_a, c_r_k, c_lnx_w, c_lnx_b,
              w_oa, w_ob, w_oc, w_out, final_norm_w):
    dt = x_prompt.dtype
    row, col = _grid_positions(x_sample.shape[1])
    xp = x_prompt
    xs = x_sample
    new_ak, new_av, new_ckv, new_kr, new_sf, new_sb = [], [], [], [], [], []
    for l in range(DEPTH):
        lp = {
            'w_in': w_in[l], 'a_qnorm_w': a_qnorm_w[l], 'a_knorm_w': a_knorm_w[l],
            'b_kvnorm_w': b_kvnorm_w[l], 'b_w_uk': b_w_uk[l], 'b_w_uv': b_w_uv[l],
            'c_mu_prev': c_mu_prev[l], 'c_mu_next': c_mu_next[l], 'c_w0': c_w0[l], 'c_w_up': c_w_up[l],
            'c_a0': c_a0[l], 'c_a_up': c_a_up[l], 'c_k_k': c_k_k[l], 'c_k_a': c_k_a[l], 'c_r_k': c_r_k[l],
            'c_lnx_w': c_lnx_w[l], 'c_lnx_b': c_lnx_b[l],
            'w_oa': w_oa[l], 'w_ob': w_ob[l], 'w_oc': w_oc[l], 'w_out': w_out[l],
        }
        shift, scale, gate = _modulation(c_ctx[None, None, :], w_mod[l], b_mod[l])
        h = _rmsnorm(xp, norm_w[l]) * (1.0 + scale) + shift
        out, ak, av, ckv, kr, s_f, s_b = _context_mixer(h, lp)
        xp = xp + gate * out
        new_ak.append(ak)
        new_av.append(av)
        new_ckv.append(ckv)
        new_kr.append(kr)
        new_sf.append(s_f.astype(dt))
        new_sb.append(s_b.astype(dt))
        shift, scale, gate = _modulation(c[:, None, :], w_mod[l], b_mod[l])
        h = _rmsnorm(xs, norm_w[l]) * (1.0 + scale) + shift
        out = _latent_mixer(h, lp, row, col, cache_a_k[:, l], cache_a_v[:, l], cache_b_ckv[:, l],
                            cache_b_krope[:, l], state_c_fwd[:, l], state_c_bwd[:, l])
        xs = xs + gate * out
    y_prompt = _rmsnorm(xp, final_norm_w)
    y_sample = _rmsnorm(xs, final_norm_w)
    new_a_k = jnp.stack(new_ak, axis=1)
    new_a_v = jnp.stack(new_av, axis=1)
    new_b_ckv = jnp.stack(new_ckv, axis=1)
    new_b_krope = jnp.stack(new_kr, axis=1)
    new_c_state_fwd = jnp.stack(new_sf, axis=1)
    new_c_state_bwd = jnp.stack(new_sb, axis=1)
    return (y_prompt, y_sample, new_a_k, new_a_v, new_b_ckv, new_b_krope, new_c_state_fwd, new_c_state_bwd)
```

```python
import functools
import math

import numpy as np
import jax
import jax.numpy as jnp
from jax import lax
from jax.experimental import pallas as pl
from jax.experimental.pallas import tpu as pltpu

F32 = jnp.float32
BF16 = jnp.bfloat16

D_MODEL = 1024
GRID_W = 64
HEAD_DIM = 64
N_HEADS = 8
A_KV_HEADS = 2
A_GROUPS = N_HEADS // A_KV_HEADS
B_NOPE = 64
B_ROPE = 32
B_KV_RANK = 128
C_WIDTH = N_HEADS * HEAD_DIM
C_LORA = 64
C_SHIFT_DIM = 3 * C_WIDTH + 4 * C_LORA
ROPE_THETA = 10000.0
NORM_EPS = 1e-6
C_GN_EPS = 64e-5

LANES = 128
SUBLANES = 8
VMEM_LIMIT_BYTES = 56 * 1024 * 1024

CHUNK = 64
N_PAIRS = N_HEADS // 2
WIDE = N_HEADS * LANES

CIN_PAD = 2048
KV_W = 4 * LANES
NF = CIN_PAD + KV_W
G_W = 3 * D_MODEL
OFF_G = 0
OFF_AQ = OFF_G + G_W
OFF_AZ = OFF_AQ + WIDE
OFF_BQ = OFF_AZ + C_WIDTH
OFF_BZ = OFF_BQ + WIDE
OFF_CZ = OFF_BZ + C_WIDTH
NB16 = OFF_CZ + C_WIDTH

_R_AQ, _R_AK, _R_AV, _R_AZ = 0, 512, 640, 768
_R_BQ, _R_CKV, _R_KR, _R_BZ = 1280, 2048, 2176, 2208
_R_CIN, _R_CZ, _R_G = 2720, 4512, 5024


def _column_maps():
    f = np.full((NF,), -1, np.int64)
    f[0:C_SHIFT_DIM] = _R_CIN + np.arange(C_SHIFT_DIM)
    f[CIN_PAD:CIN_PAD + 128] = _R_AK + np.arange(128)
    f[CIN_PAD + 128:CIN_PAD + 256] = _R_AV + np.arange(128)
    f[CIN_PAD + 256:CIN_PAD + 384] = _R_CKV + np.arange(128)
    f[CIN_PAD + 384 + 64:CIN_PAD + 384 + 96] = _R_KR + np.arange(32)
    b = np.full((NB16,), -1, np.int64)
    b[OFF_G:OFF_G + G_W] = _R_G + np.arange(G_W)
    for h in range(N_HEADS):
        kv = h // A_GROUPS
        lo = OFF_AQ + h * LANES + kv * HEAD_DIM
        b[lo:lo + HEAD_DIM] = _R_AQ + h * HEAD_DIM + np.arange(HEAD_DIM)
        lo = OFF_BQ + h * LANES
        b[lo:lo + B_NOPE + B_ROPE] = _R_BQ + h * (B_NOPE + B_ROPE) + np.arange(B_NOPE + B_ROPE)
    b[OFF_AZ:OFF_AZ + C_WIDTH] = _R_AZ + np.arange(C_WIDTH)
    b[OFF_BZ:OFF_BZ + C_WIDTH] = _R_BZ + np.arange(C_WIDTH)
    b[OFF_CZ:OFF_CZ + C_WIDTH] = _R_CZ + np.arange(C_WIDTH)
    return f, b


def _gather_cols(w, idx):
    g = jnp.take(w, jnp.asarray(np.maximum(idx, 0)), axis=-1)
    return g * jnp.asarray((idx >= 0).astype(np.float32))


def _tile(n, pref):
    t = min(n, pref)
    assert n % t == 0, (n, pref)
    return t


def _cparams(sem):
    return pltpu.CompilerParams(dimension_semantics=sem, vmem_limit_bytes=VMEM_LIMIT_BYTES)


def _silu(z):
    return z * jax.nn.sigmoid(z)


def _dot(a, b):
    return jnp.dot(a, b, preferred_element_type=F32)


def _dot_nt(a, b):
    return lax.dot_general(a, b, (((1,), (1,)), ((), ())), preferred_element_type=F32)


def _dot_tn(a, b):
    return lax.dot_general(a, b, (((0,), (0,)), ((), ())), preferred_element_type=F32)


def _split3(x):
    hi = x.astype(BF16)
    r1 = x - hi.astype(F32)
    mid = r1.astype(BF16)
    lo = (r1 - mid.astype(F32)).astype(BF16)
    return hi, mid, lo


def _dot_exact_rhs(a_bf16, x):
    hi, mid, lo = _split3(x)
    return _dot(a_bf16, hi) + _dot(a_bf16, mid) + _dot(a_bf16, lo)


def _head_sum(x, ones_bd):
    hi, mid, lo = _split3(x)
    return _dot(hi, ones_bd) + _dot(mid, ones_bd) + _dot(lo, ones_bd)


def _rope(x, cos, s1, s2, shift):
    n = x.shape[-1]
    return x * cos + pltpu.roll(x, n - shift, 1) * s1 + pltpu.roll(x, shift, 1) * s2


def _mod_kernel(cond_ref, w_ref, b_ref, o_ref):
    s = _silu(cond_ref[...]).astype(BF16)
    o_ref[0] = _dot(s, w_ref[0].astype(BF16)) + b_ref[0]


def _modulation(cond, w_mod, b_mod):
    depth, d, n = w_mod.shape
    rows = cond.shape[0]
    tn = _tile(n, 1024)
    return pl.pallas_call(
        _mod_kernel,
        grid=(depth, n // tn),
        in_specs=[pl.BlockSpec((rows, d), lambda l, j: (0, 0)),
                  pl.BlockSpec((1, d, tn), lambda l, j: (l, 0, j)),
                  pl.BlockSpec((1, 1, tn), lambda l, j: (l, 0, j))],
        out_specs=pl.BlockSpec((1, rows, tn), lambda l, j: (l, 0, j)),
        out_shape=jax.ShapeDtypeStruct((depth, rows, n), F32),
        compiler_params=_cparams(("parallel", "parallel")),
        name="modulation",
    )(cond, w_mod, b_mod.reshape(depth, 1, n))


def _proj_kernel(x_ref, sh_ref, sc_ref, nw_ref, w_ref, o_ref, h_scr):
    @pl.when(pl.program_id(2) == 0)
    def _():
        x = x_ref[0]
        ms = jnp.mean(x * x, axis=-1, keepdims=True)
        y = x * lax.rsqrt(ms + NORM_EPS) * nw_ref[...]
        h_scr[...] = (y * (1.0 + sc_ref[0]) + sh_ref[0]).astype(BF16)

    o_ref[0] = _dot(h_scr[...], w_ref[...]).astype(o_ref.dtype)


def _proj(x, shift, scale, norm_w, w, out_dtype, tn_pref):
    nb, t, d = x.shape
    n = w.shape[1]
    tm = _tile(t, 512)
    tn = _tile(n, tn_pref)
    return pl.pallas_call(
        _proj_kernel,
        grid=(nb, t // tm, n // tn),
        in_specs=[pl.BlockSpec((1, tm, d), lambda b, i, j: (b, i, 0)),
                  pl.BlockSpec((1, 1, d), lambda b, i, j: (b, 0, 0)),
                  pl.BlockSpec((1, 1, d), lambda b, i, j: (b, 0, 0)),
                  pl.BlockSpec((1, d), lambda b, i, j: (0, 0)),
                  pl.BlockSpec((d, tn), lambda b, i, j: (0, j))],
        out_specs=pl.BlockSpec((1, tm, tn), lambda b, i, j: (b, i, j)),
        out_shape=jax.ShapeDtypeStruct((nb, t, n), out_dtype),
        scratch_shapes=[pltpu.VMEM((tm, d), BF16)],
        compiler_params=_cparams(("parallel", "parallel", "arbitrary")),
        name="proj",
    )(x, shift, scale, norm_w, w)


def _kv_prep_kernel(*refs, norm, rope, emit):
    it = iter(refs)
    kv_ref = next(it)
    if rope:
        ca, sa1, sa2, cb, sb1, sb2 = (next(it)[...] for _ in range(6))
    knw_ref, cnw_ref, wuk_ref, wuv_ref = next(it), next(it), next(it), next(it)
    ka_ref, va_ref, kb_ref, vb_ref = next(it), next(it), next(it), next(it)
    if emit:
        kn_ref, cn_ref = next(it), next(it)

    ak = kv_ref[0, :, 0:LANES]
    av = kv_ref[0, :, LANES:2 * LANES]
    ckv = kv_ref[0, :, 2 * LANES:3 * LANES]
    kr = kv_ref[0, :, 3 * LANES:4 * LANES]
    lo = lax.broadcasted_iota(jnp.int32, (1, LANES), 1) < HEAD_DIM
    if norm:
        sq = ak * ak
        s0 = jnp.sum(jnp.where(lo, sq, 0.0), axis=-1, keepdims=True)
        s1 = jnp.sum(jnp.where(lo, 0.0, sq), axis=-1, keepdims=True)
        ms = jnp.where(lo, s0, s1) * (1.0 / HEAD_DIM)
        ak = ak * lax.rsqrt(ms + NORM_EPS) * knw_ref[...]
        ckv = ckv * lax.rsqrt(jnp.mean(ckv * ckv, axis=-1, keepdims=True) + NORM_EPS) * cnw_ref[...]
    if emit:
        kn_ref[0] = ak
        cn_ref[0] = ckv
    if rope:
        ak = _rope(ak, ca, sa1, sa2, 16)
        kr = _rope(kr, cb, sb1, sb2, 8)
    ka_ref[0] = ak.astype(BF16)
    av_sw = pltpu.roll(av, HEAD_DIM, 1)
    va_ref[0, :, 0:LANES] = jnp.where(lo, av, av_sw).astype(BF16)
    va_ref[0, :, LANES:2 * LANES] = jnp.where(lo, av_sw, av).astype(BF16)
    cb16 = ckv.astype(BF16)
    kn = _dot(cb16, wuk_ref[...])
    for h in range(N_HEADS):
        kb_ref[0, :, h * LANES:(h + 1) * LANES] = (kn[:, h * LANES:(h + 1) * LANES] + kr).astype(BF16)
    vb_ref[0] = _dot(cb16, wuv_ref[...]).astype(BF16)


def _kv_prep(kv, col_block, tabs, knw, cnw, wuk, wuv, *, norm, emit):
    b, t = kv.shape[0], kv.shape[1]
    tm = _tile(t, 512)
    rope = tabs is not None
    in_specs = [pl.BlockSpec((1, tm, KV_W), lambda bb, i: (bb, i, col_block))]
    args = [kv]
    if rope:
        in_specs += [pl.BlockSpec((tm, LANES), lambda bb, i: (i, 0))] * 6
        args += list(tabs)
    const = lambda bb, i: (0, 0)
    in_specs += [pl.BlockSpec((1, LANES), const), pl.BlockSpec((1, LANES), const),
                 pl.BlockSpec((B_KV_RANK, WIDE), const), pl.BlockSpec((B_KV_RANK, C_WIDTH), const)]
    args += [knw, cnw, wuk, wuv]
    widths = [(LANES, BF16), (2 * LANES, BF16), (WIDE, BF16), (C_WIDTH, BF16)]
    if emit:
        widths += [(LANES, F32), (LANES, F32)]
    out_specs = [pl.BlockSpec((1, tm, w), lambda bb, i: (bb, i, 0)) for w, _ in widths]
    out_shape = [jax.ShapeDtypeStruct((b, t, w), dt) for w, dt in widths]
    return pl.pallas_call(
        functools.partial(_kv_prep_kernel, norm=norm, rope=rope, emit=emit),
        grid=(b, t // tm),
        in_specs=in_specs, out_specs=out_specs, out_shape=out_shape,
        compiler_params=_cparams(("parallel", "parallel")),
        name="kv_prep",
    )(*args)


def _attn_kernel(*refs, mode, rope, scale):
    it = iter(refs)
    q_ref, z_ref, k_ref, v_ref = next(it), next(it), next(it), next(it)
    if rope:
        cos, s1, s2 = (next(it)[...] for _ in range(3))
    if mode == "A":
        qnw = next(it)[...]
    o_ref = next(it)

    lo = lax.broadcasted_iota(jnp.int32, (1, LANES), 1) < HEAD_DIM
    v = v_ref[0]
    outs = []
    for e in range(2):
        qh = q_ref[0, :, e * LANES:(e + 1) * LANES].astype(F32)
        if mode == "A":
            ms = jnp.sum(qh * qh, axis=-1, keepdims=True) * (1.0 / HEAD_DIM)
            qh = qh * lax.rsqrt(ms + NORM_EPS) * qnw
        if rope:
            qh = _rope(qh, cos, s1, s2, 16 if mode == "A" else 8)
        qh = (qh * scale).astype(BF16)
        kh = k_ref[0] if mode == "A" else k_ref[0, :, e * LANES:(e + 1) * LANES]
        s = _dot_nt(qh, kh)
        m = jnp.max(s, axis=-1, keepdims=True)
        p = jnp.exp(s - m)
        l = jnp.sum(p, axis=-1, keepdims=True)
        outs.append(_dot(p.astype(BF16), v) / l)
    o = jnp.where(lo, outs[0], outs[1])
    o_ref[0] = (o * _silu(z_ref[0].astype(F32))).astype(o_ref.dtype)


def _attn(ub, k, v, tabs, qnw, *, mode, q_off, z_off):
    b, tq_all = ub.shape[0], ub.shape[1]
    s_len = k.shape[1]
    tq = _tile(tq_all, 256)
    rope = tabs is not None
    qb0, zb0 = q_off // (2 * LANES), z_off // LANES
    in_specs = [pl.BlockSpec((1, tq, 2 * LANES), lambda bb, p, i: (bb, i, qb0 + p)),
                pl.BlockSpec((1, tq, LANES), lambda bb, p, i: (bb, i, zb0 + p))]
    if mode == "A":
        in_specs += [pl.BlockSpec((1, s_len, LANES), lambda bb, p, i: (bb, 0, 0)),
                     pl.BlockSpec((1, s_len, LANES), lambda bb, p, i: (bb, 0, p // 2))]
        scale = HEAD_DIM ** -0.5
    else:
        in_specs += [pl.BlockSpec((1, s_len, 2 * LANES), lambda bb, p, i: (bb, 0, p)),
                     pl.BlockSpec((1, s_len, LANES), lambda bb, p, i: (bb, 0, p))]
        scale = (B_NOPE + B_ROPE) ** -0.5
    args = [ub, ub, k, v]
    if rope:
        in_specs += [pl.BlockSpec((tq, LANES), lambda bb, p, i: (i, 0))] * 3
        args += list(tabs)
    if mode == "A":
        in_specs += [pl.BlockSpec((1, LANES), lambda bb, p, i: (0, 0))]
        args += [qnw]
    return pl.pallas_call(
        functools.partial(_attn_kernel, mode=mode, rope=rope, scale=scale),
        grid=(b, N_PAIRS, tq_all // tq),
        in_specs=in_specs,
        out_specs=pl.BlockSpec((1, tq, LANES), lambda bb, p, i: (bb, i, p)),
        out_shape=jax.ShapeDtypeStruct((b, tq_all, C_WIDTH), BF16),
        compiler_params=_cparams(("parallel", "parallel", "parallel")),
        name="attn_" + mode,
    )(*args)


def _rwkv_prep_kernel(cin_ref, hp_ref, hn_ref, mup_ref, mun_ref, w0_ref, a0_ref, wup_ref, aup_ref,
                      kk_ref, ka_ref, rk_ref, ones_ref,
                      r_o, v_o, kk_o, bonus_o, lwf_o, kf_o, bf_o, lwb_o, kb_o, bb_o):
    i = pl.program_id(1)
    n = pl.num_programs(1)
    s = cin_ref[0]
    tm = s.shape[0]
    prev_row = jnp.where(i > 0, hp_ref[0, SUBLANES - 1:SUBLANES, :], 0.0)
    next_row = jnp.where(i < n - 1, hn_ref[0, 0:1, :], 0.0)
    rows = lax.broadcasted_iota(jnp.int32, (tm, 1), 0)
    prev = jnp.where(rows == 0, prev_row, pltpu.roll(s, 1, 0))
    nxt = jnp.where(rows == tm - 1, next_row, pltpu.roll(s, tm - 1, 0))
    x = s + mup_ref[...] * (prev - s) + mun_ref[...] * (nxt - s)

    w = C_WIDTH
    r = x[:, 0:w]
    k = x[:, w:2 * w]
    v = x[:, 2 * w:3 * w]
    wd = jnp.tanh(x[:, 3 * w:3 * w + LANES]).astype(BF16)
    ad = x[:, 3 * w + LANES:3 * w + 2 * LANES].astype(BF16)
    w_raw = w0_ref[...] + _dot(wd, wup_ref[...])
    logw = (-math.exp(-0.5)) * jax.nn.sigmoid(w_raw)
    a = jax.nn.sigmoid(a0_ref[...] + _dot(ad, aup_ref[...]))
    ones_bd = ones_ref[...]
    kk = k * kk_ref[...]
    kk = kk / jnp.maximum(jnp.sqrt(_head_sum(kk * kk, ones_bd)), 1e-12)
    ka = ka_ref[...]
    a_f, a_b = a[:, 0:w], a[:, w:2 * w]
    k_f = k * (1.0 + (a_f - 1.0) * ka)
    k_b = k * (1.0 + (a_b - 1.0) * ka)
    r_o[0] = r
    v_o[0] = v
    kk_o[0] = kk
    bonus_o[0] = _head_sum(r * (k_f + k_b) * rk_ref[...], ones_bd) * v
    lwf_o[0] = logw[:, 0:w]
    kf_o[0] = k_f
    bf_o[0] = kk * a_f
    lwb_o[0] = logw[:, w:2 * w]
    kb_o[0] = k_b
    bb_o[0] = kk * a_b


def _rwkv_prep(uf, mup, mun, w0, a0, wup, aup, k_k, k_a, r_k, ones_bd):
    b, t = uf.shape[0], uf.shape[1]
    tm = _tile(t, 256)
    nblk8 = t // SUBLANES
    step8 = tm // SUBLANES
    const = lambda bb, i: (0, 0)
    in_specs = [
        pl.BlockSpec((1, tm, CIN_PAD), lambda bb, i: (bb, i, 0)),
        pl.BlockSpec((1, SUBLANES, CIN_PAD), lambda bb, i: (bb, jnp.maximum(i * step8 - 1, 0), 0)),
        pl.BlockSpec((1, SUBLANES, CIN_PAD), lambda bb, i: (bb, jnp.minimum((i + 1) * step8, nblk8 - 1), 0)),
        pl.BlockSpec((1, CIN_PAD), const), pl.BlockSpec((1, CIN_PAD), const),
        pl.BlockSpec((1, 2 * C_WIDTH), const), pl.BlockSpec((1, 2 * C_WIDTH), const),
        pl.BlockSpec((LANES, 2 * C_WIDTH), const), pl.BlockSpec((LANES, 2 * C_WIDTH), const),
        pl.BlockSpec((1, C_WIDTH), const), pl.BlockSpec((1, C_WIDTH), const), pl.BlockSpec((1, C_WIDTH), const),
        pl.BlockSpec((C_WIDTH, C_WIDTH), const),
    ]
    out_specs = [pl.BlockSpec((1, tm, C_WIDTH), lambda bb, i: (bb, i, 0))] * 10
    out_shape = [jax.ShapeDtypeStruct((b, t, C_WIDTH), F32)] * 10
    return pl.pallas_call(
        _rwkv_prep_kernel,
        grid=(b, t // tm),
        in_specs=in_specs, out_specs=out_specs, out_shape=out_shape,
        compiler_params=_cparams(("parallel", "parallel")),
        name="rwkv_prep",
    )(uf, uf, uf, mup, mun, w0, a0, wup, aup, k_k, k_a, r_k, ones_bd)


def _chunk_direction(r, v, kk, logw, kd, bd, h_scr, d_idx, y_ref, rev):
    c = CHUNK
    row_c = lax.broadcasted_iota(jnp.int32, (c, c), 0)
    col_c = lax.broadcasted_iota(jnp.int32, (c, c), 1)
    tri = (col_c >= row_c) if rev else (col_c <= row_c)
    cum = _dot_exact_rhs(jnp.where(tri, 1.0, 0.0).astype(BF16), logw)
    tot = cum[0:1, :] if rev else cum[c - 1:c, :]
    e_pos = jnp.exp(cum)
    e_neg = jnp.exp(-cum)
    a_t = -kk * jnp.exp(cum - logw)
    r_t = r * e_pos
    b_t = bd * e_neg
    k_t = kd * e_neg
    e_rem = jnp.exp(tot - cum)
    b_c = bd * e_rem
    k_c = kd * e_rem
    g_tot = jnp.exp(tot)

    lane = lax.broadcasted_iota(jnp.int32, (1, LANES), 1)
    m_lo = jnp.where(lane < HEAD_DIM, 1.0, 0.0)
    m_hi = 1.0 - m_lo
    row = lax.broadcasted_iota(jnp.int32, (LANES, LANES), 0)
    col = lax.broadcasted_iota(jnp.int32, (LANES, LANES), 1)
    same = (row // c) == (col // c)
    before = (col > row) if rev else (col < row)
    strict = same & before
    incl = same & (before | (row == col))
    diag = row == col

    for p in range(N_PAIRS):
        sl = slice(p * LANES, (p + 1) * LANES)

        def stack(z):
            zp = z[:, sl]
            return jnp.concatenate([zp * m_lo, zp * m_hi], axis=0)

        a2, r2, b2, k2, v2 = stack(a_t), stack(r_t), stack(b_t), stack(k_t), stack(v)
        bc2, kc2 = stack(b_c), stack(k_c)
        ar = jnp.concatenate([a2, r2], axis=0).astype(BF16)
        bk = jnp.concatenate([b2, k2], axis=0).astype(BF16)
        sc = _dot_nt(ar, bk)
        a_ab = jnp.where(strict, sc[0:LANES, 0:LANES], 0.0)
        a_ak = jnp.where(strict, sc[0:LANES, LANES:2 * LANES], 0.0)
        m_rb = jnp.where(incl, sc[LANES:2 * LANES, 0:LANES], 0.0)
        m_rk = jnp.where(incl, sc[LANES:2 * LANES, LANES:2 * LANES], 0.0)
        v2b = v2.astype(BF16)
        x2 = _dot(a_ak.astype(BF16), v2b)
        pw = a_ab
        t_inv = jnp.where(diag, 1.0, 0.0) + a_ab
        for _ in range(int(math.log2(c)) - 1):
            pwb = pw.astype(BF16)
            pw = _dot(pwb, pwb)
            t_inv = t_inv + _dot(t_inv.astype(BF16), pw.astype(BF16))
        g = _dot(t_inv.astype(BF16), jnp.concatenate([a2, x2], axis=1).astype(BF16))
        gb = g.astype(BF16)
        ry = jnp.concatenate([r2, _dot(m_rk.astype(BF16), v2b)], axis=1) + _dot(m_rb.astype(BF16), gb)
        pq = _dot_tn(bc2.astype(BF16), gb)
        p_mat = jnp.where(diag, g_tot[:, sl], 0.0) + pq[:, 0:LANES]
        q_mat = pq[:, LANES:2 * LANES] + _dot_tn(kc2.astype(BF16), v2b)
        h_old = h_scr[d_idx, p]
        h_hi = h_old.astype(BF16)
        h_lo = (h_old - h_hi.astype(F32)).astype(BF16)
        rw = ry[:, 0:LANES]
        rw_hi = rw.astype(BF16)
        rw_lo = (rw - rw_hi.astype(F32)).astype(BF16)
        y2 = ry[:, LANES:2 * LANES] + _dot(rw_hi, h_hi) + _dot(rw_hi, h_lo) + _dot(rw_lo, h_hi)
        p_hi = p_mat.astype(BF16)
        p_lo = (p_mat - p_hi.astype(F32)).astype(BF16)
        h_scr[d_idx, p] = q_mat + _dot(p_hi, h_hi) + _dot(p_hi, h_lo) + _dot(p_lo, h_hi)
        y_ref[0, :, sl] = y2[0:c, :] + y2[c:2 * c, :]


def _rwkv_scan_kernel(rf, vf, kkf, lwf, kf, bf, rb, vb, kkb, lwb, kb, bb, h0f, h0b,
                      yf, yb, hff, hfb, h_scr):
    i = pl.program_id(1)

    @pl.when(i == 0)
    def _():
        h_scr[0] = h0f[0]
        h_scr[1] = h0b[0]

    _chunk_direction(rf[0], vf[0], kkf[0], lwf[0], kf[0], bf[0], h_scr, 0, yf, rev=False)
    _chunk_direction(rb[0], vb[0], kkb[0], lwb[0], kb[0], bb[0], h_scr, 1, yb, rev=True)

    @pl.when(i == pl.num_programs(1) - 1)
    def _():
        hff[0] = h_scr[0]
        hfb[0] = h_scr[1]


def _rwkv_scan(prep, h0f, h0b):
    r, v, kk, _, lwf, kf, bf, lwb, kb, bb = prep
    b, t = r.shape[0], r.shape[1]
    nc = t // CHUNK
    fwd = pl.BlockSpec((1, CHUNK, C_WIDTH), lambda bb_, i: (bb_, i, 0))
    bwd = pl.BlockSpec((1, CHUNK, C_WIDTH), lambda bb_, i: (bb_, nc - 1 - i, 0))
    st = pl.BlockSpec((1, N_PAIRS, LANES, LANES), lambda bb_, i: (bb_, 0, 0, 0))
    return pl.pallas_call(
        _rwkv_scan_kernel,
        grid=(b, nc),
        in_specs=[fwd] * 6 + [bwd] * 6 + [st, st],
        out_specs=[fwd, bwd, st, st],
        out_shape=[jax.ShapeDtypeStruct((b, t, C_WIDTH), F32)] * 2
        + [jax.ShapeDtypeStruct((b, N_PAIRS, LANES, LANES), F32)] * 2,
        scratch_shapes=[pltpu.VMEM((2, N_PAIRS, LANES, LANES), F32)],
        compiler_params=_cparams(("parallel", "arbitrary")),
        name="rwkv_scan",
    )(r, v, kk, lwf, kf, bf, r, v, kk, lwb, kb, bb, h0f, h0b)


def _state_to_pairs(s):
    h = jnp.swapaxes(s.astype(F32), -1, -2).reshape(s.shape[0], N_PAIRS, 2, HEAD_DIM, HEAD_DIM)
    z = jnp.zeros_like(h[:, :, 0])
    top = jnp.concatenate([h[:, :, 0], z], axis=-1)
    bot = jnp.concatenate([z, h[:, :, 1]], axis=-1)
    return jnp.concatenate([top, bot], axis=-2)


def _pairs_to_state(hp):
    e = hp[:, :, 0:HEAD_DIM, 0:HEAD_DIM]
    o = hp[:, :, HEAD_DIM:, HEAD_DIM:]
    h = jnp.stack([e, o], axis=2).reshape(hp.shape[0], N_HEADS, HEAD_DIM, HEAD_DIM)
    return jnp.swapaxes(h, -1, -2)


def _merge_kernel(x_ref, gate_ref, ya_ref, yb_ref, yf_ref, ybw_ref, bonus_ref, cz_ref, g_ref,
                  woa_ref, wob_ref, woc_ref, wout_ref, lnw_ref, lnb_ref, ones_ref, o_ref):
    ones_bd = ones_ref[...]
    y = yf_ref[0] + ybw_ref[0]
    mu = _head_sum(y, ones_bd) * (1.0 / HEAD_DIM)
    yc = y - mu
    var = _head_sum(yc * yc, ones_bd) * (1.0 / HEAD_DIM)
    yn = yc * lax.rsqrt(var + C_GN_EPS) * lnw_ref[...] + lnb_ref[...] + bonus_ref[0]
    ycg = (yn * _silu(cz_ref[0].astype(F32))).astype(BF16)
    d = D_MODEL
    ga = jax.nn.sigmoid(g_ref[0, :, 0:d].astype(F32))
    gb = jax.nn.sigmoid(g_ref[0, :, d:2 * d].astype(F32))
    gc = jax.nn.sigmoid(g_ref[0, :, 2 * d:3 * d].astype(F32))
    mixed = (ga * _dot(ya_ref[0], woa_ref[...]) + gb * _dot(yb_ref[0], wob_ref[...])
             + gc * _dot(ycg, woc_ref[...]))
    out = _dot(mixed.astype(BF16), wout_ref[...])
    o_ref[0] = x_ref[0] + gate_ref[0] * out


def _merge(x, gate, ya, yb, yf, ybw, bonus, ub, woa, wob, woc, wout, lnw, lnb, ones_bd):
    nb, t, d = x.shape
    tm = _tile(t, 256)
    tok = lambda w: pl.BlockSpec((1, tm, w), lambda b, i: (b, i, 0))
    const = lambda b, i: (0, 0)
    in_specs = [tok(d), pl.BlockSpec((1, 1, d), lambda b, i: (b, 0, 0)),
                tok(C_WIDTH), tok(C_WIDTH), tok(C_WIDTH), tok(C_WIDTH), tok(C_WIDTH),
                pl.BlockSpec((1, tm, C_WIDTH), lambda b, i: (b, i, OFF_CZ // C_WIDTH)),
                pl.BlockSpec((1, tm, G_W), lambda b, i: (b, i, OFF_G // G_W)),
                pl.BlockSpec((C_WIDTH, d), const), pl.BlockSpec((C_WIDTH, d), const),
                pl.BlockSpec((C_WIDTH, d), const), pl.BlockSpec((d, d), const),
                pl.BlockSpec((1, C_WIDTH), const), pl.BlockSpec((1, C_WIDTH), const),
                pl.BlockSpec((C_WIDTH, C_WIDTH), const)]
    return pl.pallas_call(
        _merge_kernel,
        grid=(nb, t // tm),
        in_specs=in_specs,
        out_specs=tok(d),
        out_shape=jax.ShapeDtypeStruct((nb, t, d), F32),
        compiler_params=_cparams(("parallel", "parallel")),
        name="merge",
    )(x, gate, ya, yb, yf, ybw, bonus, ub, ub, woa, wob, woc, wout, lnw, lnb, ones_bd)


def _final_norm_kernel(x_ref, w_ref, o_ref):
    x = x_ref[...]
    o_ref[...] = x * lax.rsqrt(jnp.mean(x * x, axis=-1, keepdims=True) + NORM_EPS) * w_ref[...]


def _final_norm(x, w):
    shape = x.shape
    x2 = x.reshape(-1, shape[-1])
    n, d = x2.shape
    tm = _tile(n, 1024)
    out = pl.pallas_call(
        _final_norm_kernel,
        grid=(n // tm,),
        in_specs=[pl.BlockSpec((tm, d), lambda i: (i, 0)), pl.BlockSpec((1, d), lambda i: (0, 0))],
        out_specs=pl.BlockSpec((tm, d), lambda i: (i, 0)),
        out_shape=jax.ShapeDtypeStruct((n, d), F32),
        compiler_params=_cparams(("parallel",)),
        name="final_norm",
    )(x2, w.reshape(1, d))
    return out.reshape(shape)


def _rope_tables(n_tokens):
    t = np.arange(n_tokens)
    pos = np.stack([t // GRID_W, t % GRID_W], axis=0).astype(np.float32)
    lane = np.arange(LANES)

    def build(active, part, freq_idx, half, first):
        inv = jnp.asarray(ROPE_THETA, F32) ** (-jnp.asarray(freq_idx, F32) / half)
        posm = jnp.asarray(pos)[jnp.asarray(part)]
        ang = posm.T * inv[None, :]
        cos = jnp.where(jnp.asarray(active)[None, :], jnp.cos(ang), 1.0)
        sin = jnp.where(jnp.asarray(active)[None, :], jnp.sin(ang), 0.0)
        s1 = jnp.where(jnp.asarray(first)[None, :], -sin, 0.0)
        s2 = jnp.where(jnp.asarray(first)[None, :], 0.0, sin)
        return cos.astype(F32), s1.astype(F32), s2.astype(F32)

    i = lane % HEAD_DIM
    tab_a = build(np.ones(LANES, bool), i // 32, (i % 32) % 16, 16, (i % 32) < 16)
    i = np.clip(lane - B_NOPE, 0, B_ROPE - 1)
    active = (lane >= B_NOPE) & (lane < B_NOPE + B_ROPE)
    tab_b = build(active, i // 16, (i % 16) % 8, 8, (i % 16) < 8)
    return tab_a, tab_b


def kernel(x_prompt, x_sample, cache_a_k, cache_a_v, cache_b_ckv, cache_b_krope, state_c_fwd, state_c_bwd,
           c, c_ctx, norm_w, w_mod, b_mod, w_in, a_qnorm_w, a_knorm_w, b_kvnorm_w, b_w_uk, b_w_uv,
           c_mu_prev, c_mu_next, c_w0, c_w_up, c_a0, c_a_up, c_k_k, c_k_a, c_r_k, c_lnx_w, c_lnx_b,
           w_oa, w_ob, w_oc, w_out, final_norm_w):
    depth = w_in.shape[0]
    bc, tc, d = x_prompt.shape
    bl, tl, _ = x_sample.shape
    past = cache_a_k.shape[2]
    assert d == D_MODEL and tc % CHUNK == 0 and tl % CHUNK == 0 and tl % GRID_W == 0

    f_idx, b_idx = _column_maps()
    w_f = _gather_cols(w_in, f_idx).astype(BF16)
    w_b = _gather_cols(w_in, b_idx).astype(BF16)
    uk_idx = np.full((WIDE,), -1, np.int64)
    for h in range(N_HEADS):
        uk_idx[h * LANES:h * LANES + B_NOPE] = h * B_NOPE + np.arange(B_NOPE)
    wuk = _gather_cols(b_w_uk, uk_idx).astype(BF16)
    wuv = b_w_uv.astype(BF16)
    knw = jnp.tile(a_knorm_w, (1, 2)).reshape(depth, 1, LANES)
    qnw = jnp.tile(a_qnorm_w, (1, 2)).reshape(depth, 1, LANES)
    cnw = b_kvnorm_w.reshape(depth, 1, LANES)
    pad_c = CIN_PAD - C_SHIFT_DIM
    mup = jnp.pad(c_mu_prev, ((0, 0), (0, pad_c))).reshape(depth, 1, CIN_PAD)
    mun = jnp.pad(c_mu_next, ((0, 0), (0, pad_c))).reshape(depth, 1, CIN_PAD)
    w0 = c_w0.reshape(depth, 1, 2 * C_WIDTH)
    a0 = c_a0.reshape(depth, 1, 2 * C_WIDTH)

    def lora_stack(up):
        z = jnp.zeros_like(up[:, 0])
        top = jnp.concatenate([up[:, 0], z], axis=-1)
        bot = jnp.concatenate([z, up[:, 1]], axis=-1)
        return jnp.concatenate([top, bot], axis=-2).astype(BF16)

    wup, aup = lora_stack(c_w_up), lora_stack(c_a_up)
    k_k = c_k_k.reshape(depth, 1, C_WIDTH)
    k_a = c_k_a.reshape(depth, 1, C_WIDTH)
    r_k = c_r_k.reshape(depth, 1, C_WIDTH)
    lnw = c_lnx_w.reshape(depth, 1, C_WIDTH)
    lnb = c_lnx_b.reshape(depth, 1, C_WIDTH)
    woa, wob, woc, wout = (w.astype(BF16) for w in (w_oa, w_ob, w_oc, w_out))
    hid = np.arange(C_WIDTH) // HEAD_DIM
    ones_bd = jnp.asarray((hid[:, None] == hid[None, :]).astype(np.float32)).astype(BF16)
    tab_a, tab_b = _rope_tables(tl)

    rows = -(-(bl + 1) // SUBLANES) * SUBLANES
    cond = jnp.zeros((rows, d), F32).at[0:bl].set(c).at[bl].set(c_ctx)
    mod = _modulation(cond, w_mod, b_mod)

    def mod_parts(l, lo, hi):
        m = mod[l, lo:hi]
        return tuple(m[:, j * d:(j + 1) * d].reshape(hi - lo, 1, d) for j in range(3))

    zeros_state = jnp.zeros((bc, N_PAIRS, LANES, LANES), F32)
    cache_kr_pad = jnp.pad(cache_b_krope, ((0, 0), (0, 0), (0, 0), (B_NOPE, LANES - B_NOPE - B_ROPE)))
    cache_kv = jnp.concatenate([cache_a_k.reshape(bl, depth, past, LANES),
                                cache_a_v.reshape(bl, depth, past, LANES),
                                cache_b_ckv, cache_kr_pad], axis=-1)

    xp = x_prompt.reshape(1, bc * tc, d)
    xs = x_sample
    new_ak, new_av, new_ckv, new_kr, new_sf, new_sb = [], [], [], [], [], []

    def mixer_tail(x, gate, ub, uf, ka, va, kb, vb, tabs_q, h0f, h0b, l, nb_tok):
        b_, t_ = ub.shape[0], ub.shape[1]
        ya = _attn(ub, ka, va, tabs_q[0], qnw[l], mode="A", q_off=OFF_AQ, z_off=OFF_AZ)
        yb = _attn(ub, kb, vb, tabs_q[1], None, mode="B", q_off=OFF_BQ, z_off=OFF_BZ)
        prep = _rwkv_prep(uf, mup[l], mun[l], w0[l], a0[l], wup[l], aup[l], k_k[l], k_a[l], r_k[l], ones_bd)
        yf, ybw, hff, hfb = _rwkv_scan(prep, h0f, h0b)
        rs = lambda z: z.reshape(nb_tok, (b_ * t_) // nb_tok, z.shape[-1])
        x_new = _merge(x, gate, rs(ya), rs(yb), rs(yf), rs(ybw), rs(prep[3]), rs(ub),
                       woa[l], wob[l], woc[l], wout[l], lnw[l], lnb[l], ones_bd)
        return x_new, hff, hfb

    for l in range(depth):
        nw = norm_w[l].reshape(1, d)
        shift, scale, gate = mod_parts(l, bl, bl + 1)
        uf = _proj(xp, shift, scale, nw, w_f[l], F32, 512).reshape(bc, tc, NF)
        ub = _proj(xp, shift, scale, nw, w_b[l], BF16, 512).reshape(bc, tc, NB16)
        ka, va, kb, vb, k_n, ckv_n = _kv_prep(uf, CIN_PAD // KV_W, None, knw[l], cnw[l], wuk[l], wuv[l],
                                               norm=True, emit=True)
        xp, hff, hfb = mixer_tail(xp, gate, ub, uf, ka, va, kb, vb, (None, None), zeros_state, zeros_state, l, 1)
        new_ak.append(k_n.reshape(bc, tc, A_KV_HEADS, HEAD_DIM))
        new_av.append(uf[:, :, CIN_PAD + LANES:CIN_PAD + 2 * LANES].reshape(bc, tc, A_KV_HEADS, HEAD_DIM))
        new_ckv.append(ckv_n)
        new_kr.append(uf[:, :, CIN_PAD + 3 * LANES + B_NOPE:CIN_PAD + 3 * LANES + B_NOPE + B_ROPE])
        new_sf.append(_pairs_to_state(hff))
        new_sb.append(_pairs_to_state(hfb))
        shift, scale, gate = mod_parts(l, 0, bl)
        uf = _proj(xs, shift, scale, nw, w_f[l], F32, 512)
        ub = _proj(xs, shift, scale, nw, w_b[l], BF16, 512)
        ka, va, kb, vb = _kv_prep(uf, CIN_PAD // KV_W, tab_a + tab_b, knw[l], cnw[l], wuk[l], wuv[l],
                                  norm=True, emit=False)
        cka, cva, ckb, cvb = _kv_prep(cache_kv[:, l], 0, None, knw[l], cnw[l], wuk[l], wuv[l],
                                      norm=False, emit=False)
        ka, va, kb, vb = (jnp.concatenate([x_, y_], axis=1) for x_, y_ in
                          ((ka, cka), (va, cva), (kb, ckb), (vb, cvb)))
        xs, _, _ = mixer_tail(xs, gate, ub, uf, ka, va, kb, vb, (tab_a, tab_b),
                              _state_to_pairs(state_c_fwd[:, l]), _state_to_pairs(state_c_bwd[:, l]), l, bl)

    y_prompt = _final_norm(xp, final_norm_w).reshape(bc, tc, d)
    y_sample = _final_norm(xs, final_norm_w)
    return (y_prompt, y_sample,
            jnp.stack(new_ak, axis=1), jnp.stack(new_av, axis=1),
            jnp.stack(new_ckv, axis=1), jnp.stack(new_kr, axis=1),
            jnp.stack(new_sf, axis=1), jnp.stack(new_sb, axis=1))
```

```python
import functools
import math

import numpy as np
import jax
import jax.numpy as jnp
from jax import lax
from jax.experimental import pallas as pl
from jax.experimental.pallas import tpu as pltpu

F32 = jnp.float32
BF16 = jnp.bfloat16

D_MODEL = 1024
GRID_W = 64
HEAD_DIM = 64
N_HEADS = 8
A_KV_HEADS = 2
A_GROUPS = N_HEADS // A_KV_HEADS
B_NOPE = 64
B_ROPE = 32
B_KV_RANK = 128
C_WIDTH = N_HEADS * HEAD_DIM
C_LORA = 64
C_SHIFT_DIM = 3 * C_WIDTH + 4 * C_LORA
ROPE_THETA = 10000.0
NORM_EPS = 1e-6
C_GN_EPS = 64e-5
LOG2_E = math.log2(math.e)

LANES = 128
SUBLANES = 8
VMEM_LIMIT_BYTES = 56 * 1024 * 1024

ATTN_ROW_SPLIT = 2
CHUNK = 64
N_PAIRS = N_HEADS // 2
WIDE = N_HEADS * LANES

CIN_PAD = 2048
KV_W = 4 * LANES
NF = CIN_PAD + KV_W
G_W = 3 * D_MODEL
OFF_G = 0
OFF_AQ = OFF_G + G_W
OFF_AZ = OFF_AQ + WIDE
OFF_BQ = OFF_AZ + C_WIDTH
OFF_BZ = OFF_BQ + WIDE
OFF_CZ = OFF_BZ + C_WIDTH
NB16 = OFF_CZ + C_WIDTH

_R_AQ, _R_AK, _R_AV, _R_AZ = 0, 512, 640, 768
_R_BQ, _R_CKV, _R_KR, _R_BZ = 1280, 2048, 2176, 2208
_R_CIN, _R_CZ, _R_G = 2720, 4512, 5024


def _column_maps():
    f = np.full((NF,), -1, np.int64)
    f[0:C_SHIFT_DIM] = _R_CIN + np.arange(C_SHIFT_DIM)
    f[CIN_PAD:CIN_PAD + 128] = _R_AK + np.arange(128)
    f[CIN_PAD + 128:CIN_PAD + 256] = _R_AV + np.arange(128)
    f[CIN_PAD + 256:CIN_PAD + 384] = _R_CKV + np.arange(128)
    f[CIN_PAD + 384 + 64:CIN_PAD + 384 + 96] = _R_KR + np.arange(32)
    b = np.full((NB16,), -1, np.int64)
    b[OFF_G:OFF_G + G_W] = _R_G + np.arange(G_W)
    for h in range(N_HEADS):
        kv = h // A_GROUPS
        lo = OFF_AQ + h * LANES + kv * HEAD_DIM
        b[lo:lo + HEAD_DIM] = _R_AQ + h * HEAD_DIM + np.arange(HEAD_DIM)
        lo = OFF_BQ + h * LANES
        b[lo:lo + B_NOPE + B_ROPE] = _R_BQ + h * (B_NOPE + B_ROPE) + np.arange(B_NOPE + B_ROPE)
    b[OFF_AZ:OFF_AZ + C_WIDTH] = _R_AZ + np.arange(C_WIDTH)
    b[OFF_BZ:OFF_BZ + C_WIDTH] = _R_BZ + np.arange(C_WIDTH)
    b[OFF_CZ:OFF_CZ + C_WIDTH] = _R_CZ + np.arange(C_WIDTH)
    return f, b


def _gather_cols(w, idx):
    g = jnp.take(w, jnp.asarray(np.maximum(idx, 0)), axis=-1)
    return g * jnp.asarray((idx >= 0).astype(np.float32))


def _tile(n, pref):
    t = min(n, pref)
    assert n % t == 0, (n, pref)
    return t


def _cparams(sem):
    return pltpu.CompilerParams(dimension_semantics=sem, vmem_limit_bytes=VMEM_LIMIT_BYTES)


def _silu(z):
    return z * jax.nn.sigmoid(z)


def _dot(a, b):
    return jnp.dot(a, b, preferred_element_type=F32)


def _dot_nt(a, b):
    return lax.dot_general(a, b, (((1,), (1,)), ((), ())), preferred_element_type=F32)


def _dot_tn(a, b):
    return lax.dot_general(a, b, (((0,), (0,)), ((), ())), preferred_element_type=F32)


def _split3(x):
    hi = x.astype(BF16)
    r1 = x - hi.astype(F32)
    mid = r1.astype(BF16)
    lo = (r1 - mid.astype(F32)).astype(BF16)
    return hi, mid, lo


def _dot_exact_rhs(a_bf16, x):
    hi, mid, lo = _split3(x)
    return _dot(a_bf16, hi) + _dot(a_bf16, mid) + _dot(a_bf16, lo)


def _head_sum(x, ones_bd):
    hi, mid, lo = _split3(x)
    return _dot(hi, ones_bd) + _dot(mid, ones_bd) + _dot(lo, ones_bd)


def _rope(x, cos, s1, s2, shift):
    n = x.shape[-1]
    return x * cos + pltpu.roll(x, n - shift, 1) * s1 + pltpu.roll(x, shift, 1) * s2


def _mod_kernel(cond_ref, w_ref, b_ref, o_ref):
    s = _silu(cond_ref[...]).astype(BF16)
    o_ref[0] = _dot(s, w_ref[0].astype(BF16)) + b_ref[0]


def _modulation(cond, w_mod, b_mod):
    depth, d, n = w_mod.shape
    rows = cond.shape[0]
    tn = _tile(n, 1024)
    return pl.pallas_call(
        _mod_kernel,
        grid=(depth, n // tn),
        in_specs=[pl.BlockSpec((rows, d), lambda l, j: (0, 0)),
                  pl.BlockSpec((1, d, tn), lambda l, j: (l, 0, j)),
                  pl.BlockSpec((1, 1, tn), lambda l, j: (l, 0, j))],
        out_specs=pl.BlockSpec((1, rows, tn), lambda l, j: (l, 0, j)),
        out_shape=jax.ShapeDtypeStruct((depth, rows, n), F32),
        compiler_params=_cparams(("parallel", "parallel")),
        name="modulation",
    )(cond, w_mod, b_mod.reshape(depth, 1, n))


def _proj_kernel(x_ref, sh_ref, sc_ref, nw_ref, w_ref, o_ref, h_scr):
    @pl.when(pl.program_id(2) == 0)
    def _():
        x = x_ref[0]
        ms = jnp.mean(x * x, axis=-1, keepdims=True)
        y = x * lax.rsqrt(ms + NORM_EPS) * nw_ref[...]
        h_scr[...] = (y * (1.0 + sc_ref[0]) + sh_ref[0]).astype(BF16)

    o_ref[0] = _dot(h_scr[...], w_ref[...]).astype(o_ref.dtype)


def _proj(x, shift, scale, norm_w, w, out_dtype, tn_pref):
    nb, t, d = x.shape
    n = w.shape[1]
    tm = _tile(t, 512)
    tn = _tile(n, tn_pref)
    return pl.pallas_call(
        _proj_kernel,
        grid=(nb, t // tm, n // tn),
        in_specs=[pl.BlockSpec((1, tm, d), lambda b, i, j: (b, i, 0)),
                  pl.BlockSpec((1, 1, d), lambda b, i, j: (b, 0, 0)),
                  pl.BlockSpec((1, 1, d), lambda b, i, j: (b, 0, 0)),
                  pl.BlockSpec((1, d), lambda b, i, j: (0, 0)),
                  pl.BlockSpec((d, tn), lambda b, i, j: (0, j))],
        out_specs=pl.BlockSpec((1, tm, tn), lambda b, i, j: (b, i, j)),
        out_shape=jax.ShapeDtypeStruct((nb, t, n), out_dtype),
        scratch_shapes=[pltpu.VMEM((tm, d), BF16)],
        compiler_params=_cparams(("parallel", "parallel", "arbitrary")),
        name="proj",
    )(x, shift, scale, norm_w, w)


def _kv_prep_kernel(*refs, norm, rope, emit):
    it = iter(refs)
    kv_ref = next(it)
    if rope:
        ca, sa1, sa2, cb, sb1, sb2 = (next(it)[...] for _ in range(6))
    knw_ref, cnw_ref, wuk_ref, wuv_ref = next(it), next(it), next(it), next(it)
    ka_ref, va_ref, kb_ref, vb_ref = next(it), next(it), next(it), next(it)
    if emit:
        kn_ref, cn_ref = next(it), next(it)

    ak = kv_ref[0, :, 0:LANES]
    av = kv_ref[0, :, LANES:2 * LANES]
    ckv = kv_ref[0, :, 2 * LANES:3 * LANES]
    kr = kv_ref[0, :, 3 * LANES:4 * LANES]
    lo = lax.broadcasted_iota(jnp.int32, (1, LANES), 1) < HEAD_DIM
    if norm:
        sq = ak * ak
        s0 = jnp.sum(jnp.where(lo, sq, 0.0), axis=-1, keepdims=True)
        s1 = jnp.sum(jnp.where(lo, 0.0, sq), axis=-1, keepdims=True)
        ms = jnp.where(lo, s0, s1) * (1.0 / HEAD_DIM)
        ak = ak * lax.rsqrt(ms + NORM_EPS) * knw_ref[...]
        ckv = ckv * lax.rsqrt(jnp.mean(ckv * ckv, axis=-1, keepdims=True) + NORM_EPS) * cnw_ref[...]
    if emit:
        kn_ref[0] = ak
        cn_ref[0] = ckv
    if rope:
        ak = _rope(ak, ca, sa1, sa2, 16)
        kr = _rope(kr, cb, sb1, sb2, 8)
    ka_ref[0] = ak.astype(BF16)
    av_sw = pltpu.roll(av, HEAD_DIM, 1)
    va_ref[0, :, 0:LANES] = jnp.where(lo, av, 1.0).astype(BF16)
    va_ref[0, :, LANES:2 * LANES] = jnp.where(lo, 1.0, av_sw).astype(BF16)
    va_ref[0, :, 2 * LANES:3 * LANES] = jnp.where(lo, av_sw, 1.0).astype(BF16)
    va_ref[0, :, 3 * LANES:4 * LANES] = jnp.where(lo, 1.0, av).astype(BF16)
    cb16 = ckv.astype(BF16)
    kn = _dot(cb16, wuk_ref[...])
    for h in range(N_HEADS):
        kb_ref[0, :, h * LANES:(h + 1) * LANES] = (kn[:, h * LANES:(h + 1) * LANES] + kr).astype(BF16)
    vn = _dot(cb16, wuv_ref[...])
    for p in range(N_PAIRS):
        vp = vn[:, p * LANES:(p + 1) * LANES]
        vb_ref[0, :, 2 * p * LANES:(2 * p + 1) * LANES] = jnp.where(lo, vp, 1.0).astype(BF16)
        vb_ref[0, :, (2 * p + 1) * LANES:(2 * p + 2) * LANES] = jnp.where(lo, 1.0, vp).astype(BF16)


def _kv_prep(kv, col_block, tabs, knw, cnw, wuk, wuv, *, norm, emit):
    b, t = kv.shape[0], kv.shape[1]
    tm = _tile(t, 512)
    rope = tabs is not None
    in_specs = [pl.BlockSpec((1, tm, KV_W), lambda bb, i: (bb, i, col_block))]
    args = [kv]
    if rope:
        in_specs += [pl.BlockSpec((tm, LANES), lambda bb, i: (i, 0))] * 6
        args += list(tabs)
    const = lambda bb, i: (0, 0)
    in_specs += [pl.BlockSpec((1, LANES), const), pl.BlockSpec((1, LANES), const),
                 pl.BlockSpec((B_KV_RANK, WIDE), const), pl.BlockSpec((B_KV_RANK, C_WIDTH), const)]
    args += [knw, cnw, wuk, wuv]
    widths = [(LANES, BF16), (4 * LANES, BF16), (WIDE, BF16), (WIDE, BF16)]
    if emit:
        widths += [(LANES, F32), (LANES, F32)]
    out_specs = [pl.BlockSpec((1, tm, w), lambda bb, i: (bb, i, 0)) for w, _ in widths]
    out_shape = [jax.ShapeDtypeStruct((b, t, w), dt) for w, dt in widths]
    return pl.pallas_call(
        functools.partial(_kv_prep_kernel, norm=norm, rope=rope, emit=emit),
        grid=(b, t // tm),
        in_specs=in_specs, out_specs=out_specs, out_shape=out_shape,
        compiler_params=_cparams(("parallel", "parallel")),
        name="kv_prep",
    )(*args)


def _attn_kernel(*refs, mode, rope, scale):
    it = iter(refs)
    q_ref, z_ref, k_ref, v_ref = next(it), next(it), next(it), next(it)
    if rope:
        cos, s1, s2 = (next(it)[...] for _ in range(3))
    if mode == "A":
        qnw = next(it)[...]
    o_ref = next(it)

    lo = lax.broadcasted_iota(jnp.int32, (1, LANES), 1) < HEAD_DIM
    tq = q_ref.shape[1]
    rows = tq // ATTN_ROW_SPLIT
    chains = [(e, r) for e in range(2) for r in range(ATTN_ROW_SPLIT)]
    st = {}
    for e, r in chains:
        rs = slice(r * rows, (r + 1) * rows)
        qh = q_ref[0, rs, e * LANES:(e + 1) * LANES].astype(F32)
        if mode == "A":
            ms = jnp.sum(qh * qh, axis=-1, keepdims=True) * (1.0 / HEAD_DIM)
            qh = qh * lax.rsqrt(ms + NORM_EPS) * qnw
        if rope:
            qh = _rope(qh, cos[rs], s1[rs], s2[rs], 16 if mode == "A" else 8)
        qh = (qh * (scale * LOG2_E)).astype(BF16)
        kh = k_ref[0] if mode == "A" else k_ref[0, :, e * LANES:(e + 1) * LANES]
        st[e, r] = _dot_nt(qh, kh)
    for c in chains:
        st[c] = (st[c], jnp.max(st[c], axis=-1, keepdims=True))
    for c in chains:
        s, m = st[c]
        st[c] = jnp.exp2(s - m).astype(BF16)
    for e, r in chains:
        st[e, r] = _dot(st[e, r], v_ref[0])[:, e * LANES:(e + 1) * LANES]
    for r in range(ATTN_ROW_SPLIT):
        rs = slice(r * rows, (r + 1) * rows)
        oe, oo = st[0, r], st[1, r]
        o = jnp.where(lo, oe, oo)
        l = jnp.where(lo, pltpu.roll(oe, HEAD_DIM, 1), pltpu.roll(oo, HEAD_DIM, 1))
        o_ref[0, rs, :] = (o / l * _silu(z_ref[0, rs, :].astype(F32))).astype(o_ref.dtype)


def _attn(ub, k, v, tabs, qnw, *, mode, q_off, z_off):
    b, tq_all = ub.shape[0], ub.shape[1]
    s_len = k.shape[1]
    tq = _tile(tq_all, 256)
    rope = tabs is not None
    qb0, zb0 = q_off // (2 * LANES), z_off // LANES
    in_specs = [pl.BlockSpec((1, tq, 2 * LANES), lambda bb, p, i: (bb, i, qb0 + p)),
                pl.BlockSpec((1, tq, LANES), lambda bb, p, i: (bb, i, zb0 + p))]
    if mode == "A":
        in_specs += [pl.BlockSpec((1, s_len, LANES), lambda bb, p, i: (bb, 0, 0)),
                     pl.BlockSpec((1, s_len, 2 * LANES), lambda bb, p, i: (bb, 0, p // 2))]
        scale = HEAD_DIM ** -0.5
    else:
        in_specs += [pl.BlockSpec((1, s_len, 2 * LANES), lambda bb, p, i: (bb, 0, p)),
                     pl.BlockSpec((1, s_len, 2 * LANES), lambda bb, p, i: (bb, 0, p))]
        scale = (B_NOPE + B_ROPE) ** -0.5
    args = [ub, ub, k, v]
    if rope:
        in_specs += [pl.BlockSpec((tq, LANES), lambda bb, p, i: (i, 0))] * 3
        args += list(tabs)
    if mode == "A":
        in_specs += [pl.BlockSpec((1, LANES), lambda bb, p, i: (0, 0))]
        args += [qnw]
    return pl.pallas_call(
        functools.partial(_attn_kernel, mode=mode, rope=rope, scale=scale),
        grid=(b, N_PAIRS, tq_all // tq),
        in_specs=in_specs,
        out_specs=pl.BlockSpec((1, tq, LANES), lambda bb, p, i: (bb, i, p)),
        out_shape=jax.ShapeDtypeStruct((b, tq_all, C_WIDTH), BF16),
        compiler_params=_cparams(("parallel", "parallel", "parallel")),
        name="attn_" + mode,
    )(*args)


def _rwkv_prep_kernel(cin_ref, hp_ref, hn_ref, mup_ref, mun_ref, w0_ref, a0_ref, wup_ref, aup_ref,
                      kk_ref, ka_ref, rk_ref, ones_ref,
                      r_o, v_o, kk_o, bonus_o, lwf_o, kf_o, bf_o, lwb_o, kb_o, bb_o):
    i = pl.program_id(1)
    n = pl.num_programs(1)
    s = cin_ref[0]
    tm = s.shape[0]
    prev_row = jnp.where(i > 0, hp_ref[0, SUBLANES - 1:SUBLANES, :], 0.0)
    next_row = jnp.where(i < n - 1, hn_ref[0, 0:1, :], 0.0)
    rows = lax.broadcasted_iota(jnp.int32, (tm, 1), 0)
    prev = jnp.where(rows == 0, prev_row, pltpu.roll(s, 1, 0))
    nxt = jnp.where(rows == tm - 1, next_row, pltpu.roll(s, tm - 1, 0))
    x = s + mup_ref[...] * (prev - s) + mun_ref[...] * (nxt - s)

    w = C_WIDTH
    r = x[:, 0:w]
    k = x[:, w:2 * w]
    v = x[:, 2 * w:3 * w]
    wd = jnp.tanh(x[:, 3 * w:3 * w + LANES]).astype(BF16)
    ad = x[:, 3 * w + LANES:3 * w + 2 * LANES].astype(BF16)
    w_raw = w0_ref[...] + _dot(wd, wup_ref[...])
    logw = (-math.exp(-0.5)) * jax.nn.sigmoid(w_raw)
    a = jax.nn.sigmoid(a0_ref[...] + _dot(ad, aup_ref[...]))
    ones_bd = ones_ref[...]
    kk = k * kk_ref[...]
    kk = kk / jnp.maximum(jnp.sqrt(_head_sum(kk * kk, ones_bd)), 1e-12)
    ka = ka_ref[...]
    a_f, a_b = a[:, 0:w], a[:, w:2 * w]
    k_f = k * (1.0 + (a_f - 1.0) * ka)
    k_b = k * (1.0 + (a_b - 1.0) * ka)
    r_o[0] = r
    v_o[0] = v
    kk_o[0] = kk
    bonus_o[0] = _head_sum(r * (k_f + k_b) * rk_ref[...], ones_bd) * v
    lwf_o[0] = logw[:, 0:w]
    kf_o[0] = k_f
    bf_o[0] = kk * a_f
    lwb_o[0] = logw[:, w:2 * w]
    kb_o[0] = k_b
    bb_o[0] = kk * a_b


def _rwkv_prep(uf, mup, mun, w0, a0, wup, aup, k_k, k_a, r_k, ones_bd):
    b, t = uf.shape[0], uf.shape[1]
    tm = _tile(t, 256)
    nblk8 = t // SUBLANES
    step8 = tm // SUBLANES
    const = lambda bb, i: (0, 0)
    in_specs = [
        pl.BlockSpec((1, tm, CIN_PAD), lambda bb, i: (bb, i, 0)),
        pl.BlockSpec((1, SUBLANES, CIN_PAD), lambda bb, i: (bb, jnp.maximum(i * step8 - 1, 0), 0)),
        pl.BlockSpec((1, SUBLANES, CIN_PAD), lambda bb, i: (bb, jnp.minimum((i + 1) * step8, nblk8 - 1), 0)),
        pl.BlockSpec((1, CIN_PAD), const), pl.BlockSpec((1, CIN_PAD), const),
        pl.BlockSpec((1, 2 * C_WIDTH), const), pl.BlockSpec((1, 2 * C_WIDTH), const),
        pl.BlockSpec((LANES, 2 * C_WIDTH), const), pl.BlockSpec((LANES, 2 * C_WIDTH), const),
        pl.BlockSpec((1, C_WIDTH), const), pl.BlockSpec((1, C_WIDTH), const), pl.BlockSpec((1, C_WIDTH), const),
        pl.BlockSpec((C_WIDTH, C_WIDTH), const),
    ]
    out_specs = [pl.BlockSpec((1, tm, C_WIDTH), lambda bb, i: (bb, i, 0))] * 10
    out_shape = [jax.ShapeDtypeStruct((b, t, C_WIDTH), F32)] * 10
    return pl.pallas_call(
        _rwkv_prep_kernel,
        grid=(b, t // tm),
        in_specs=in_specs, out_specs=out_specs, out_shape=out_shape,
        compiler_params=_cparams(("parallel", "parallel")),
        name="rwkv_prep",
    )(uf, uf, uf, mup, mun, w0, a0, wup, aup, k_k, k_a, r_k, ones_bd)


def _chunk_operands(r, v, kk, logw, kd, bd, rev):
    c = CHUNK
    rows = lax.broadcasted_iota(jnp.int32, (c, 1), 0)
    cum = logw
    shift = 1
    while shift < c:
        if rev:
            cum = cum + jnp.where(rows < c - shift, pltpu.roll(cum, c - shift, 0), 0.0)
        else:
            cum = cum + jnp.where(rows >= shift, pltpu.roll(cum, shift, 0), 0.0)
        shift *= 2
    tot = cum[0:1, :] if rev else cum[c - 1:c, :]
    e_neg = jnp.exp(-cum)
    e_rem = jnp.exp(tot - cum)
    return dict(a=-kk * jnp.exp(cum - logw), r=r * jnp.exp(cum), b=bd * e_neg, k=kd * e_neg,
                bc=bd * e_rem, kc=kd * e_rem, v=v, g_tot=jnp.exp(tot))


def _scan_chunk(dirs, h_scr, y_refs):
    c = CHUNK
    lane = lax.broadcasted_iota(jnp.int32, (1, LANES), 1)
    m_lo = jnp.where(lane < HEAD_DIM, 1.0, 0.0)
    m_hi = 1.0 - m_lo
    row = lax.broadcasted_iota(jnp.int32, (LANES, LANES), 0)
    col = lax.broadcasted_iota(jnp.int32, (LANES, LANES), 1)
    same = (row // c) == (col // c)
    diag = row == col
    eye = jnp.where(diag, 1.0, 0.0)

    chains = []
    for d_idx, (ops, rev) in enumerate(dirs):
        before = (col > row) if rev else (col < row)
        strict = same & before
        incl = same & (before | diag)
        for p in range(N_PAIRS):
            sl = slice(p * LANES, (p + 1) * LANES)

            def stack(z, sl=sl):
                zp = z[:, sl]
                return jnp.concatenate([zp * m_lo, zp * m_hi], axis=0)

            ch = {name: stack(ops[name]) for name in ("a", "r", "b", "k", "bc", "kc", "v")}
            ch.update(d=d_idx, sl=sl, strict=strict, incl=incl, g_tot=ops["g_tot"][:, sl])
            chains.append(ch)

    for ch in chains:
        ar = jnp.concatenate([ch["a"], ch["r"]], axis=0).astype(BF16)
        bk = jnp.concatenate([ch["b"], ch["k"]], axis=0).astype(BF16)
        ch["sc"] = _dot_nt(ar, bk)
    for ch in chains:
        sc = ch.pop("sc")
        ch["a_ab"] = jnp.where(ch["strict"], sc[0:LANES, 0:LANES], 0.0)
        a_ak = jnp.where(ch["strict"], sc[0:LANES, LANES:2 * LANES], 0.0)
        ch["m_rb"] = jnp.where(ch["incl"], sc[LANES:2 * LANES, 0:LANES], 0.0).astype(BF16)
        m_rk = jnp.where(ch["incl"], sc[LANES:2 * LANES, LANES:2 * LANES], 0.0)
        ch["vb"] = ch["v"].astype(BF16)
        xm = _dot(jnp.concatenate([a_ak, m_rk], axis=0).astype(BF16), ch["vb"])
        ch["x"] = xm[0:LANES]
        ch["mv"] = xm[LANES:2 * LANES]
        ch["kv"] = _dot_tn(ch["kc"].astype(BF16), ch["vb"])
        ch["t"] = eye + ch["a_ab"]
    for ch in chains:
        ab = ch.pop("a_ab").astype(BF16)
        ch["pw"] = _dot(ab, ab)
    n_levels = int(math.log2(c)) - 1
    for lvl in range(n_levels):
        last = lvl == n_levels - 1
        for ch in chains:
            pwb = ch["pw"].astype(BF16)
            tb = ch["t"].astype(BF16)
            if last:
                ch["t"] = ch["t"] + _dot(pwb, tb)
            else:
                nxt = _dot(pwb, jnp.concatenate([pwb, tb], axis=1))
                ch["pw"] = nxt[:, 0:LANES]
                ch["t"] = ch["t"] + nxt[:, LANES:2 * LANES]
    for ch in chains:
        ax = jnp.concatenate([ch["a"], ch["x"]], axis=1).astype(BF16)
        ch["g"] = _dot(ch["t"].astype(BF16), ax).astype(BF16)
    for ch in chains:
        ch["ry"] = jnp.concatenate([ch["r"], ch["mv"]], axis=1) + _dot(ch["m_rb"], ch["g"])
        ch["pq"] = _dot_tn(ch["bc"].astype(BF16), ch["g"])
    for ch in chains:
        q_mat = ch["pq"][:, LANES:2 * LANES] + ch["kv"]
        d_idx, p = ch["d"], ch["sl"].start // LANES
        h_old = h_scr[d_idx, p]
        rp = jnp.concatenate([ch["ry"][:, 0:LANES], ch["pq"][:, 0:LANES]], axis=0).astype(BF16)
        yh = _dot(rp, h_old.astype(BF16))
        y2 = ch["ry"][:, LANES:2 * LANES] + yh[0:LANES]
        g_rows = jnp.transpose(jnp.broadcast_to(ch["g_tot"], (LANES, LANES)))
        h_scr[d_idx, p] = g_rows * h_old + yh[LANES:2 * LANES] + q_mat
        y_refs[d_idx][0, :, ch["sl"]] = y2[0:c, :] + y2[c:2 * c, :]


def _rwkv_scan_kernel(rf, vf, kkf, lwf, kf, bf, rb, vb, kkb, lwb, kb, bb, h0f, h0b,
                      yf, yb, hff, hfb, h_scr):
    i = pl.program_id(1)

    @pl.when(i == 0)
    def _():
        h_scr[0] = h0f[0]
        h_scr[1] = h0b[0]

    fwd = _chunk_operands(rf[0], vf[0], kkf[0], lwf[0], kf[0], bf[0], rev=False)
    bwd = _chunk_operands(rb[0], vb[0], kkb[0], lwb[0], kb[0], bb[0], rev=True)
    _scan_chunk([(fwd, False), (bwd, True)], h_scr, (yf, yb))

    @pl.when(i == pl.num_programs(1) - 1)
    def _():
        hff[0] = h_scr[0]
        hfb[0] = h_scr[1]


def _rwkv_scan(prep, h0f, h0b):
    r, v, kk, _, lwf, kf, bf, lwb, kb, bb = prep
    b, t = r.shape[0], r.shape[1]
    nc = t // CHUNK
    fwd = pl.BlockSpec((1, CHUNK, C_WIDTH), lambda bb_, i: (bb_, i, 0))
    bwd = pl.BlockSpec((1, CHUNK, C_WIDTH), lambda bb_, i: (bb_, nc - 1 - i, 0))
    st = pl.BlockSpec((1, N_PAIRS, LANES, LANES), lambda bb_, i: (bb_, 0, 0, 0))
    return pl.pallas_call(
        _rwkv_scan_kernel,
        grid=(b, nc),
        in_specs=[fwd] * 6 + [bwd] * 6 + [st, st],
        out_specs=[fwd, bwd, st, st],
        out_shape=[jax.ShapeDtypeStruct((b, t, C_WIDTH), F32)] * 2
        + [jax.ShapeDtypeStruct((b, N_PAIRS, LANES, LANES), F32)] * 2,
        scratch_shapes=[pltpu.VMEM((2, N_PAIRS, LANES, LANES), F32)],
        compiler_params=_cparams(("parallel", "arbitrary")),
        name="rwkv_scan",
    )(r, v, kk, lwf, kf, bf, r, v, kk, lwb, kb, bb, h0f, h0b)


def _state_to_pairs(s):
    h = jnp.swapaxes(s.astype(F32), -1, -2).reshape(s.shape[0], N_PAIRS, 2, HEAD_DIM, HEAD_DIM)
    z = jnp.zeros_like(h[:, :, 0])
    top = jnp.concatenate([h[:, :, 0], z], axis=-1)
    bot = jnp.concatenate([z, h[:, :, 1]], axis=-1)
    return jnp.concatenate([top, bot], axis=-2)


def _pairs_to_state(hp):
    e = hp[:, :, 0:HEAD_DIM, 0:HEAD_DIM]
    o = hp[:, :, HEAD_DIM:, HEAD_DIM:]
    h = jnp.stack([e, o], axis=2).reshape(hp.shape[0], N_HEADS, HEAD_DIM, HEAD_DIM)
    return jnp.swapaxes(h, -1, -2)


def _merge_kernel(x_ref, gate_ref, ya_ref, yb_ref, yf_ref, ybw_ref, bonus_ref, cz_ref, g_ref,
                  woa_ref, wob_ref, woc_ref, wout_ref, lnw_ref, lnb_ref, ones_ref, o_ref):
    ones_bd = ones_ref[...]
    y = yf_ref[0] + ybw_ref[0]
    mu = _head_sum(y, ones_bd) * (1.0 / HEAD_DIM)
    yc = y - mu
    var = _head_sum(yc * yc, ones_bd) * (1.0 / HEAD_DIM)
    yn = yc * lax.rsqrt(var + C_GN_EPS) * lnw_ref[...] + lnb_ref[...] + bonus_ref[0]
    ycg = (yn * _silu(cz_ref[0].astype(F32))).astype(BF16)
    d = D_MODEL
    ga = jax.nn.sigmoid(g_ref[0, :, 0:d].astype(F32))
    gb = jax.nn.sigmoid(g_ref[0, :, d:2 * d].astype(F32))
    gc = jax.nn.sigmoid(g_ref[0, :, 2 * d:3 * d].astype(F32))
    mixed = (ga * _dot(ya_ref[0], woa_ref[...]) + gb * _dot(yb_ref[0], wob_ref[...])
             + gc * _dot(ycg, woc_ref[...]))
    out = _dot(mixed.astype(BF16), wout_ref[...])
    o_ref[0] = x_ref[0] + gate_ref[0] * out


def _merge(x, gate, ya, yb, yf, ybw, bonus, ub, woa, wob, woc, wout, lnw, lnb, ones_bd):
    nb, t, d = x.shape
    tm = _tile(t, 256)
    tok = lambda w: pl.BlockSpec((1, tm, w), lambda b, i: (b, i, 0))
    const = lambda b, i: (0, 0)
    in_specs = [tok(d), pl.BlockSpec((1, 1, d), lambda b, i: (b, 0, 0)),
                tok(C_WIDTH), tok(C_WIDTH), tok(C_WIDTH), tok(C_WIDTH), tok(C_WIDTH),
                pl.BlockSpec((1, tm, C_WIDTH), lambda b, i: (b, i, OFF_CZ // C_WIDTH)),
                pl.BlockSpec((1, tm, G_W), lambda b, i: (b, i, OFF_G // G_W)),
                pl.BlockSpec((C_WIDTH, d), const), pl.BlockSpec((C_WIDTH, d), const),
                pl.BlockSpec((C_WIDTH, d), const), pl.BlockSpec((d, d), const),
                pl.BlockSpec((1, C_WIDTH), const), pl.BlockSpec((1, C_WIDTH), const),
                pl.BlockSpec((C_WIDTH, C_WIDTH), const)]
    return pl.pallas_call(
        _merge_kernel,
        grid=(nb, t // tm),
        in_specs=in_specs,
        out_specs=tok(d),
        out_shape=jax.ShapeDtypeStruct((nb, t, d), F32),
        compiler_params=_cparams(("parallel", "parallel")),
        name="merge",
    )(x, gate, ya, yb, yf, ybw, bonus, ub, ub, woa, wob, woc, wout, lnw, lnb, ones_bd)


def _final_norm_kernel(x_ref, w_ref, o_ref):
    x = x_ref[...]
    o_ref[...] = x * lax.rsqrt(jnp.mean(x * x, axis=-1, keepdims=True) + NORM_EPS) * w_ref[...]


def _final_norm(x, w):
    shape = x.shape
    x2 = x.reshape(-1, shape[-1])
    n, d = x2.shape
    tm = _tile(n, 1024)
    out = pl.pallas_call(
        _final_norm_kernel,
        grid=(n // tm,),
        in_specs=[pl.BlockSpec((tm, d), lambda i: (i, 0)), pl.BlockSpec((1, d), lambda i: (0, 0))],
        out_specs=pl.BlockSpec((tm, d), lambda i: (i, 0)),
        out_shape=jax.ShapeDtypeStruct((n, d), F32),
        compiler_params=_cparams(("parallel",)),
        name="final_norm",
    )(x2, w.reshape(1, d))
    return out.reshape(shape)


def _rope_tables(n_tokens):
    t = np.arange(n_tokens)
    pos = np.stack([t // GRID_W, t % GRID_W], axis=0).astype(np.float32)
    lane = np.arange(LANES)

    def build(active, part, freq_idx, half, first):
        inv = jnp.asarray(ROPE_THETA, F32) ** (-jnp.asarray(freq_idx, F32) / half)
        posm = jnp.asarray(pos)[jnp.asarray(part)]
        ang = posm.T * inv[None, :]
        cos = jnp.where(jnp.asarray(active)[None, :], jnp.cos(ang), 1.0)
        sin = jnp.where(jnp.asarray(active)[None, :], jnp.sin(ang), 0.0)
        s1 = jnp.where(jnp.asarray(first)[None, :], -sin, 0.0)
        s2 = jnp.where(jnp.asarray(first)[None, :], 0.0, sin)
        return cos.astype(F32), s1.astype(F32), s2.astype(F32)

    i = lane % HEAD_DIM
    tab_a = build(np.ones(LANES, bool), i // 32, (i % 32) % 16, 16, (i % 32) < 16)
    i = np.clip(lane - B_NOPE, 0, B_ROPE - 1)
    active = (lane >= B_NOPE) & (lane < B_NOPE + B_ROPE)
    tab_b = build(active, i // 16, (i % 16) % 8, 8, (i % 16) < 8)
    return tab_a, tab_b


def kernel(x_prompt, x_sample, cache_a_k, cache_a_v, cache_b_ckv, cache_b_krope, state_c_fwd, state_c_bwd,
           c, c_ctx, norm_w, w_mod, b_mod, w_in, a_qnorm_w, a_knorm_w, b_kvnorm_w, b_w_uk, b_w_uv,
           c_mu_prev, c_mu_next, c_w0, c_w_up, c_a0, c_a_up, c_k_k, c_k_a, c_r_k, c_lnx_w, c_lnx_b,
           w_oa, w_ob, w_oc, w_out, final_norm_w):
    depth = w_in.shape[0]
    bc, tc, d = x_prompt.shape
    bl, tl, _ = x_sample.shape
    past = cache_a_k.shape[2]
    assert d == D_MODEL and tc % CHUNK == 0 and tl % CHUNK == 0 and tl % GRID_W == 0

    f_idx, b_idx = _column_maps()
    w_f = _gather_cols(w_in, f_idx).astype(BF16)
    w_b = _gather_cols(w_in, b_idx).astype(BF16)
    uk_idx = np.full((WIDE,), -1, np.int64)
    for h in range(N_HEADS):
        uk_idx[h * LANES:h * LANES + B_NOPE] = h * B_NOPE + np.arange(B_NOPE)
    wuk = _gather_cols(b_w_uk, uk_idx).astype(BF16)
    wuv = b_w_uv.astype(BF16)
    knw = jnp.tile(a_knorm_w, (1, 2)).reshape(depth, 1, LANES)
    qnw = jnp.tile(a_qnorm_w, (1, 2)).reshape(depth, 1, LANES)
    cnw = b_kvnorm_w.reshape(depth, 1, LANES)
    pad_c = CIN_PAD - C_SHIFT_DIM
    mup = jnp.pad(c_mu_prev, ((0, 0), (0, pad_c))).reshape(depth, 1, CIN_PAD)
    mun = jnp.pad(c_mu_next, ((0, 0), (0, pad_c))).reshape(depth, 1, CIN_PAD)
    w0 = c_w0.reshape(depth, 1, 2 * C_WIDTH)
    a0 = c_a0.reshape(depth, 1, 2 * C_WIDTH)

    def lora_stack(up):
        z = jnp.zeros_like(up[:, 0])
        top = jnp.concatenate([up[:, 0], z], axis=-1)
        bot = jnp.concatenate([z, up[:, 1]], axis=-1)
        return jnp.concatenate([top, bot], axis=-2).astype(BF16)

    wup, aup = lora_stack(c_w_up), lora_stack(c_a_up)
    k_k = c_k_k.reshape(depth, 1, C_WIDTH)
    k_a = c_k_a.reshape(depth, 1, C_WIDTH)
    r_k = c_r_k.reshape(depth, 1, C_WIDTH)
    lnw = c_lnx_w.reshape(depth, 1, C_WIDTH)
    lnb = c_lnx_b.reshape(depth, 1, C_WIDTH)
    woa, wob, woc, wout = (w.astype(BF16) for w in (w_oa, w_ob, w_oc, w_out))
    hid = np.arange(C_WIDTH) // HEAD_DIM
    ones_bd = jnp.asarray((hid[:, None] == hid[None, :]).astype(np.float32)).astype(BF16)
    tab_a, tab_b = _rope_tables(tl)

    rows = -(-(bl + 1) // SUBLANES) * SUBLANES
    cond = jnp.zeros((rows, d), F32).at[0:bl].set(c).at[bl].set(c_ctx)
    mod = _modulation(cond, w_mod, b_mod)

    def mod_parts(l, lo, hi):
        m = mod[l, lo:hi]
        return tuple(m[:, j * d:(j + 1) * d].reshape(hi - lo, 1, d) for j in range(3))

    zeros_state = jnp.zeros((bc, N_PAIRS, LANES, LANES), F32)
    cache_kr_pad = jnp.pad(cache_b_krope, ((0, 0), (0, 0), (0, 0), (B_NOPE, LANES - B_NOPE - B_ROPE)))
    cache_kv = jnp.concatenate([cache_a_k.reshape(bl, depth, past, LANES),
                                cache_a_v.reshape(bl, depth, past, LANES),
                                cache_b_ckv, cache_kr_pad], axis=-1)

    xp = x_prompt.reshape(1, bc * tc, d)
    xs = x_sample
    new_ak, new_av, new_ckv, new_kr, new_sf, new_sb = [], [], [], [], [], []

    def mixer_tail(x, gate, ub, uf, ka, va, kb, vb, tabs_q, h0f, h0b, l, nb_tok):
        b_, t_ = ub.shape[0], ub.shape[1]
        ya = _attn(ub, ka, va, tabs_q[0], qnw[l], mode="A", q_off=OFF_AQ, z_off=OFF_AZ)
        yb = _attn(ub, kb, vb, tabs_q[1], None, mode="B", q_off=OFF_BQ, z_off=OFF_BZ)
        prep = _rwkv_prep(uf, mup[l], mun[l], w0[l], a0[l], wup[l], aup[l], k_k[l], k_a[l], r_k[l], ones_bd)
        yf, ybw, hff, hfb = _rwkv_scan(prep, h0f, h0b)
        rs = lambda z: z.reshape(nb_tok, (b_ * t_) // nb_tok, z.shape[-1])
        x_new = _merge(x, gate, rs(ya), rs(yb), rs(yf), rs(ybw), rs(prep[3]), rs(ub),
                       woa[l], wob[l], woc[l], wout[l], lnw[l], lnb[l], ones_bd)
        return x_new, hff, hfb

    for l in range(depth):
        nw = norm_w[l].reshape(1, d)
        shift, scale, gate = mod_parts(l, bl, bl + 1)
        uf = _proj(xp, shift, scale, nw, w_f[l], F32, 512).reshape(bc, tc, NF)
        ub = _proj(xp, shift, scale, nw, w_b[l], BF16, 512).reshape(bc, tc, NB16)
        ka, va, kb, vb, k_n, ckv_n = _kv_prep(uf, CIN_PAD // KV_W, None, knw[l], cnw[l], wuk[l], wuv[l],
                                               norm=True, emit=True)
        xp, hff, hfb = mixer_tail(xp, gate, ub, uf, ka, va, kb, vb, (None, None), zeros_state, zeros_state, l, 1)
        new_ak.append(k_n.reshape(bc, tc, A_KV_HEADS, HEAD_DIM))
        new_av.append(uf[:, :, CIN_PAD + LANES:CIN_PAD + 2 * LANES].reshape(bc, tc, A_KV_HEADS, HEAD_DIM))
        new_ckv.append(ckv_n)
        new_kr.append(uf[:, :, CIN_PAD + 3 * LANES + B_NOPE:CIN_PAD + 3 * LANES + B_NOPE + B_ROPE])
        new_sf.append(_pairs_to_state(hff))
        new_sb.append(_pairs_to_state(hfb))
        shift, scale, gate = mod_parts(l, 0, bl)
        uf = _proj(xs, shift, scale, nw, w_f[l], F32, 512)
        ub = _proj(xs, shift, scale, nw, w_b[l], BF16, 512)
        ka, va, kb, vb = _kv_prep(uf, CIN_PAD // KV_W, tab_a + tab_b, knw[l], cnw[l], wuk[l], wuv[l],
                                  norm=True, emit=False)
        cka, cva, ckb, cvb = _kv_prep(cache_kv[:, l], 0, None, knw[l], cnw[l], wuk[l], wuv[l],
                                      norm=False, emit=False)
        ka, va, kb, vb = (jnp.concatenate([x_, y_], axis=1) for x_, y_ in
                          ((ka, cka), (va, cva), (kb, ckb), (vb, cvb)))
        xs, _, _ = mixer_tail(xs, gate, ub, uf, ka, va, kb, vb, (tab_a, tab_b),
                              _state_to_pairs(state_c_fwd[:, l]), _state_to_pairs(state_c_bwd[:, l]), l, bl)

    y_prompt = _final_norm(xp, final_norm_w).reshape(bc, tc, d)
    y_sample = _final_norm(xs, final_norm_w)
    return (y_prompt, y_sample,
            jnp.stack(new_ak, axis=1), jnp.stack(new_av, axis=1),
            jnp.stack(new_ckv, axis=1), jnp.stack(new_kr, axis=1),
            jnp.stack(new_sf, axis=1), jnp.stack(new_sb, axis=1))
```

```python
import functools
import math

import numpy as np
import jax
import jax.numpy as jnp
from jax import lax
from jax.experimental import pallas as pl
from jax.experimental.pallas import tpu as pltpu

F32 = jnp.float32
BF16 = jnp.bfloat16

D_MODEL = 1024
GRID_W = 64
HEAD_DIM = 64
N_HEADS = 8
A_KV_HEADS = 2
A_GROUPS = N_HEADS // A_KV_HEADS
B_NOPE = 64
B_ROPE = 32
B_KV_RANK = 128
C_WIDTH = N_HEADS * HEAD_DIM
C_LORA = 64
C_SHIFT_DIM = 3 * C_WIDTH + 4 * C_LORA
ROPE_THETA = 10000.0
NORM_EPS = 1e-6
C_GN_EPS = 64e-5
LOG2_E = math.log2(math.e)

LANES = 128
SUBLANES = 8
VMEM_LIMIT_BYTES = 56 * 1024 * 1024

PROJ_COL_CHUNK = 512
ATTN_KEY_CHUNK = 256
ATTN_SKEW = 2
VT_ROWS = HEAD_DIM + 16
CHUNK = 64
N_PAIRS = N_HEADS // 2
WIDE = N_HEADS * LANES

CIN_PAD = 2048
KV_W = 4 * LANES
NF = CIN_PAD + KV_W
G_W = 3 * D_MODEL
OFF_G = 0
OFF_AQ = OFF_G + G_W
OFF_AZ = OFF_AQ + WIDE
OFF_BQ = OFF_AZ + C_WIDTH
OFF_BZ = OFF_BQ + WIDE
OFF_CZ = OFF_BZ + C_WIDTH
NB16 = OFF_CZ + C_WIDTH

_R_AQ, _R_AK, _R_AV, _R_AZ = 0, 512, 640, 768
_R_BQ, _R_CKV, _R_KR, _R_BZ = 1280, 2048, 2176, 2208
_R_CIN, _R_CZ, _R_G = 2720, 4512, 5024


def _column_maps():
    f = np.full((NF,), -1, np.int64)
    f[0:C_SHIFT_DIM] = _R_CIN + np.arange(C_SHIFT_DIM)
    f[CIN_PAD:CIN_PAD + 128] = _R_AK + np.arange(128)
    f[CIN_PAD + 128:CIN_PAD + 256] = _R_AV + np.arange(128)
    f[CIN_PAD + 256:CIN_PAD + 384] = _R_CKV + np.arange(128)
    f[CIN_PAD + 384 + 64:CIN_PAD + 384 + 96] = _R_KR + np.arange(32)
    b = np.full((NB16,), -1, np.int64)
    b[OFF_G:OFF_G + G_W] = _R_G + np.arange(G_W)
    for h in range(N_HEADS):
        kv = h // A_GROUPS
        lo = OFF_AQ + h * LANES + kv * HEAD_DIM
        b[lo:lo + HEAD_DIM] = _R_AQ + h * HEAD_DIM + np.arange(HEAD_DIM)
        lo = OFF_BQ + h * LANES
        b[lo:lo + B_NOPE + B_ROPE] = _R_BQ + h * (B_NOPE + B_ROPE) + np.arange(B_NOPE + B_ROPE)
    b[OFF_AZ:OFF_AZ + C_WIDTH] = _R_AZ + np.arange(C_WIDTH)
    b[OFF_BZ:OFF_BZ + C_WIDTH] = _R_BZ + np.arange(C_WIDTH)
    b[OFF_CZ:OFF_CZ + C_WIDTH] = _R_CZ + np.arange(C_WIDTH)
    return f, b


def _gather_cols(w, idx):
    g = jnp.take(w, jnp.asarray(np.maximum(idx, 0)), axis=-1)
    return g * jnp.asarray((idx >= 0).astype(np.float32))


def _tile(n, pref):
    t = min(n, pref)
    assert n % t == 0, (n, pref)
    return t


def _cparams(sem):
    return pltpu.CompilerParams(dimension_semantics=sem, vmem_limit_bytes=VMEM_LIMIT_BYTES)


def _silu(z):
    return z * jax.nn.sigmoid(z)


def _dot(a, b):
    return jnp.dot(a, b, preferred_element_type=F32)


def _dot_nt(a, b):
    return lax.dot_general(a, b, (((1,), (1,)), ((), ())), preferred_element_type=F32)


def _dot_tn(a, b):
    return lax.dot_general(a, b, (((0,), (0,)), ((), ())), preferred_element_type=F32)


def _head_sum(x, ones_bd):
    hi = x.astype(BF16)
    lo = (x - hi.astype(F32)).astype(BF16)
    w = ones_bd.shape[0]
    halves = [_dot(hi[:, c:c + w], ones_bd) + _dot(lo[:, c:c + w], ones_bd) for c in range(0, x.shape[1], w)]
    return jnp.concatenate(halves, axis=1)


def _rope(x, cos, s1, s2, shift):
    n = x.shape[-1]
    return x * cos + pltpu.roll(x, n - shift, 1) * s1 + pltpu.roll(x, shift, 1) * s2


def _mod_kernel(cond_ref, w_ref, b_ref, o_ref):
    s = _silu(cond_ref[...]).astype(BF16)
    o_ref[0] = _dot(s, w_ref[0].astype(BF16)) + b_ref[0]


def _modulation(cond, w_mod, b_mod):
    depth, d, n = w_mod.shape
    rows = cond.shape[0]
    tn = _tile(n, 1024)
    return pl.pallas_call(
        _mod_kernel,
        grid=(depth, n // tn),
        in_specs=[pl.BlockSpec((rows, d), lambda l, j: (0, 0)),
                  pl.BlockSpec((1, d, tn), lambda l, j: (l, 0, j)),
                  pl.BlockSpec((1, 1, tn), lambda l, j: (l, 0, j))],
        out_specs=pl.BlockSpec((1, rows, tn), lambda l, j: (l, 0, j)),
        out_shape=jax.ShapeDtypeStruct((depth, rows, n), F32),
        compiler_params=_cparams(("parallel", "parallel")),
        name="modulation",
    )(cond, w_mod, b_mod.reshape(depth, 1, n))


def _proj_kernel(x_ref, sh_ref, sc_ref, nw_ref, w_ref, of_ref, ob_ref):
    x = x_ref[0]
    ms = jnp.mean(x * x, axis=-1, keepdims=True)
    y = x * lax.rsqrt(ms + NORM_EPS) * nw_ref[...]
    h = (y * (1.0 + sc_ref[0]) + sh_ref[0]).astype(BF16)
    for c0 in range(0, NF, PROJ_COL_CHUNK):
        of_ref[0, :, c0:c0 + PROJ_COL_CHUNK] = _dot(h, w_ref[:, c0:c0 + PROJ_COL_CHUNK])
    for c0 in range(0, NB16, PROJ_COL_CHUNK):
        ob_ref[0, :, c0:c0 + PROJ_COL_CHUNK] = _dot(h, w_ref[:, NF + c0:NF + c0 + PROJ_COL_CHUNK]).astype(BF16)


def _proj(x, shift, scale, norm_w, w):
    nb, t, d = x.shape
    tm = _tile(t, 256)
    return pl.pallas_call(
        _proj_kernel,
        grid=(nb, t // tm),
        in_specs=[pl.BlockSpec((1, tm, d), lambda b, i: (b, i, 0)),
                  pl.BlockSpec((1, 1, d), lambda b, i: (b, 0, 0)),
                  pl.BlockSpec((1, 1, d), lambda b, i: (b, 0, 0)),
                  pl.BlockSpec((1, d), lambda b, i: (0, 0)),
                  pl.BlockSpec((d, NF + NB16), lambda b, i: (0, 0), pipeline_mode=pl.Buffered(1))],
        out_specs=[pl.BlockSpec((1, tm, NF), lambda b, i: (b, i, 0)),
                   pl.BlockSpec((1, tm, NB16), lambda b, i: (b, i, 0))],
        out_shape=[jax.ShapeDtypeStruct((nb, t, NF), F32), jax.ShapeDtypeStruct((nb, t, NB16), BF16)],
        compiler_params=_cparams(("parallel", "parallel")),
        name="proj",
    )(x, shift, scale, norm_w, w)


def _kv_prep_tile(kv, tabs, knw, cnw, wuk, wuv, outs, cache_outs, *, norm):
    ka_ref, va_ref, kb_ref, vb_ref = outs
    ak = kv[:, 0:LANES]
    av = kv[:, LANES:2 * LANES]
    ckv = kv[:, 2 * LANES:3 * LANES]
    kr = kv[:, 3 * LANES:4 * LANES]
    lo = lax.broadcasted_iota(jnp.int32, (1, LANES), 1) < HEAD_DIM
    if norm:
        sq = ak * ak
        s0 = jnp.sum(jnp.where(lo, sq, 0.0), axis=-1, keepdims=True)
        s1 = jnp.sum(jnp.where(lo, 0.0, sq), axis=-1, keepdims=True)
        ms = jnp.where(lo, s0, s1) * (1.0 / HEAD_DIM)
        ak = ak * lax.rsqrt(ms + NORM_EPS) * knw
        ckv = ckv * lax.rsqrt(jnp.mean(ckv * ckv, axis=-1, keepdims=True) + NORM_EPS) * cnw
    if cache_outs is not None:
        cache_outs[0][0] = ak
        cache_outs[1][0] = ckv
    if tabs is not None:
        ca, sa1, sa2, cb, sb1, sb2 = tabs
        ak = _rope(ak, ca, sa1, sa2, 16)
        kr = _rope(kr, cb, sb1, sb2, 8)
    ka_ref[0] = ak.astype(BF16)
    tm = kv.shape[0]
    ones_rows = jnp.ones((VT_ROWS - HEAD_DIM, tm), BF16)

    def put_pair(ref, h0, pair):
        pt = jnp.transpose(pair).astype(BF16)
        for e in range(2):
            ref[0, h0 + e, 0:HEAD_DIM, :] = pt[e * HEAD_DIM:(e + 1) * HEAD_DIM]
            ref[0, h0 + e, HEAD_DIM:VT_ROWS, :] = ones_rows

    put_pair(va_ref, 0, av)
    cb16 = ckv.astype(BF16)
    kn = _dot(cb16, wuk)
    for h in range(N_HEADS):
        kb_ref[0, :, h * LANES:(h + 1) * LANES] = (kn[:, h * LANES:(h + 1) * LANES] + kr).astype(BF16)
    vn = _dot(cb16, wuv)
    for p in range(N_PAIRS):
        put_pair(vb_ref, 2 * p, vn[:, p * LANES:(p + 1) * LANES])


def _kv_prep_kernel(*refs, rope, emit, n_new):
    it = iter(refs)
    kv_ref = next(it)
    cache_ref = next(it) if n_new is not None else None
    tabs = tuple(next(it)[...] for _ in range(6)) if rope else None
    knw, cnw, wuk, wuv = (next(it)[...] for _ in range(4))
    outs = (next(it), next(it), next(it), next(it))
    cache_outs = (next(it), next(it)) if emit else None
    if n_new is None:
        _kv_prep_tile(kv_ref[0], tabs, knw, cnw, wuk, wuv, outs, cache_outs, norm=True)
        return
    i = pl.program_id(1)

    @pl.when(i < n_new)
    def _():
        _kv_prep_tile(kv_ref[0], tabs, knw, cnw, wuk, wuv, outs, cache_outs, norm=True)

    @pl.when(i >= n_new)
    def _():
        _kv_prep_tile(cache_ref[0], None, knw, cnw, wuk, wuv, outs, None, norm=False)


def _kv_prep(kv, col_block, cache, tabs, knw, cnw, wuk, wuv, *, emit):
    b, t = kv.shape[0], kv.shape[1]
    rope = tabs is not None
    if cache is None:
        tm, n_new, total = _tile(t, 512), None, t
        in_specs = [pl.BlockSpec((1, tm, KV_W), lambda bb, i: (bb, i, col_block))]
        args = [kv]
        tab_spec = pl.BlockSpec((tm, LANES), lambda bb, i: (i, 0))
    else:
        past = cache.shape[1]
        tm = _tile(past, 512)
        assert t % tm == 0
        n_new, total = t // tm, t + past
        in_specs = [pl.BlockSpec((1, tm, KV_W), lambda bb, i: (bb, jnp.minimum(i, n_new - 1), col_block)),
                    pl.BlockSpec((1, tm, KV_W), lambda bb, i: (bb, jnp.maximum(i - n_new, 0), 0))]
        args = [kv, cache]
        tab_spec = pl.BlockSpec((tm, LANES), lambda bb, i: (jnp.minimum(i, n_new - 1), 0))
    if rope:
        in_specs += [tab_spec] * 6
        args += list(tabs)
    const = lambda bb, i: (0, 0)
    in_specs += [pl.BlockSpec((1, LANES), const), pl.BlockSpec((1, LANES), const),
                 pl.BlockSpec((B_KV_RANK, WIDE), const), pl.BlockSpec((B_KV_RANK, C_WIDTH), const)]
    args += [knw, cnw, wuk, wuv]
    def rows_out(w, dt):
        return pl.BlockSpec((1, tm, w), lambda bb, i: (bb, i, 0)), jax.ShapeDtypeStruct((b, total, w), dt)

    def vt_out(nh):
        return (pl.BlockSpec((1, nh, VT_ROWS, tm), lambda bb, i: (bb, 0, 0, i)),
                jax.ShapeDtypeStruct((b, nh, VT_ROWS, total), BF16))

    outs = [rows_out(LANES, BF16), vt_out(A_KV_HEADS), rows_out(WIDE, BF16), vt_out(N_HEADS)]
    if emit:
        outs += [rows_out(LANES, F32), rows_out(LANES, F32)]
    out_specs = [o[0] for o in outs]
    out_shape = [o[1] for o in outs]
    return pl.pallas_call(
        functools.partial(_kv_prep_kernel, rope=rope, emit=emit, n_new=n_new),
        grid=(b, total // tm),
        in_specs=in_specs, out_specs=out_specs, out_shape=out_shape,
        compiler_params=_cparams(("parallel", "parallel")),
        name="kv_prep",
    )(*args)


def _attn_kernel(*refs, mode, rope, scale):
    it = iter(refs)
    q_ref, z_ref, k_ref, vt_ref = next(it), next(it), next(it), next(it)
    if rope:
        cos, s1, s2 = (next(it)[...] for _ in range(3))
    if mode == "A":
        qnw = next(it)[...]
    o_ref = next(it)

    tq = q_ref.shape[1]
    s_len = k_ref.shape[1]
    ck = ATTN_KEY_CHUNK if s_len % ATTN_KEY_CHUNK == 0 else LANES
    n_chunks = s_len // ck
    heads = (0, 1)
    qs = {}
    for e in heads:
        qh = q_ref[0, :, e * LANES:(e + 1) * LANES].astype(F32)
        if mode == "A":
            ms = jnp.sum(qh * qh, axis=-1, keepdims=True) * (1.0 / HEAD_DIM)
            qh = qh * lax.rsqrt(ms + NORM_EPS) * qnw
        if rope:
            qh = _rope(qh, cos, s1, s2, 16 if mode == "A" else 8)
        qs[e] = (qh * (scale * LOG2_E)).astype(BF16)

    m = {e: jnp.full((1, tq), -1e30, F32) for e in heads}
    acc = {e: jnp.zeros((VT_ROWS, tq), F32) for e in heads}
    st = {}

    def scores(e, t):
        kh = k_ref[0, t * ck:(t + 1) * ck, :] if mode == "A" else k_ref[0, t * ck:(t + 1) * ck, e * LANES:(e + 1) * LANES]
        st[e, t] = _dot_nt(kh, qs[e])

    def consume(e, j):
        s = st.pop((e, j))
        m_new = jnp.maximum(m[e], jnp.max(s, axis=0, keepdims=True))
        p = jnp.exp2(s - m_new).astype(BF16)
        vt = vt_ref[0, 0 if mode == "A" else e, :, j * ck:(j + 1) * ck]
        acc[e] = acc[e] * jnp.exp2(m[e] - m_new) + _dot(vt, p)
        m[e] = m_new

    for t in range(n_chunks + ATTN_SKEW):
        if t < n_chunks:
            for e in heads:
                scores(e, t)
        if t >= ATTN_SKEW:
            for e in heads:
                consume(e, t - ATTN_SKEW)
    o_t = jnp.concatenate([acc[e][0:HEAD_DIM] / acc[e][HEAD_DIM:HEAD_DIM + 1] for e in heads], axis=0)
    o_ref[0] = (jnp.transpose(o_t) * _silu(z_ref[0].astype(F32))).astype(o_ref.dtype)


def _attn(ub, k, v, tabs, qnw, *, mode, q_off, z_off):
    b, tq_all = ub.shape[0], ub.shape[1]
    s_len = k.shape[1]
    tq = _tile(tq_all, 256)
    rope = tabs is not None
    qb0, zb0 = q_off // (2 * LANES), z_off // LANES
    in_specs = [pl.BlockSpec((1, tq, 2 * LANES), lambda bb, p, i: (bb, i, qb0 + p)),
                pl.BlockSpec((1, tq, LANES), lambda bb, p, i: (bb, i, zb0 + p))]
    if mode == "A":
        in_specs += [pl.BlockSpec((1, s_len, LANES), lambda bb, p, i: (bb, 0, 0)),
                     pl.BlockSpec((1, 1, VT_ROWS, s_len), lambda bb, p, i: (bb, p // 2, 0, 0))]
        scale = HEAD_DIM ** -0.5
    else:
        in_specs += [pl.BlockSpec((1, s_len, 2 * LANES), lambda bb, p, i: (bb, 0, p)),
                     pl.BlockSpec((1, 2, VT_ROWS, s_len), lambda bb, p, i: (bb, p, 0, 0))]
        scale = (B_NOPE + B_ROPE) ** -0.5
    args = [ub, ub, k, v]
    if rope:
        in_specs += [pl.BlockSpec((tq, LANES), lambda bb, p, i: (i, 0))] * 3
        args += list(tabs)
    if mode == "A":
        in_specs += [pl.BlockSpec((1, LANES), lambda bb, p, i: (0, 0))]
        args += [qnw]
    return pl.pallas_call(
        functools.partial(_attn_kernel, mode=mode, rope=rope, scale=scale),
        grid=(b, N_PAIRS, tq_all // tq),
        in_specs=in_specs,
        out_specs=pl.BlockSpec((1, tq, LANES), lambda bb, p, i: (bb, i, p)),
        out_shape=jax.ShapeDtypeStruct((b, tq_all, C_WIDTH), BF16),
        compiler_params=_cparams(("parallel", "parallel", "parallel")),
        name="attn_" + mode,
    )(*args)


def _rwkv_prep_kernel(cin_ref, hp_ref, hn_ref, mup_ref, mun_ref, w0_ref, a0_ref, wup_ref, aup_ref,
                      kk_ref, ka_ref, rk_ref, ones_ref,
                      r_o, v_o, kk_o, bonus_o, lwf_o, kf_o, bf_o, lwb_o, kb_o, bb_o):
    i = pl.program_id(1)
    n = pl.num_programs(1)
    s = cin_ref[0]
    tm = s.shape[0]
    prev_row = jnp.where(i > 0, hp_ref[0, SUBLANES - 1:SUBLANES, :], 0.0)
    next_row = jnp.where(i < n - 1, hn_ref[0, 0:1, :], 0.0)
    rows = lax.broadcasted_iota(jnp.int32, (tm, 1), 0)
    prev = jnp.where(rows == 0, prev_row, pltpu.roll(s, 1, 0))
    nxt = jnp.where(rows == tm - 1, next_row, pltpu.roll(s, tm - 1, 0))
    x = s + mup_ref[...] * (prev - s) + mun_ref[...] * (nxt - s)

    w = C_WIDTH
    r = x[:, 0:w]
    k = x[:, w:2 * w]
    v = x[:, 2 * w:3 * w]
    wd = jnp.tanh(x[:, 3 * w:3 * w + LANES]).astype(BF16)
    ad = x[:, 3 * w + LANES:3 * w + 2 * LANES].astype(BF16)
    w_raw = w0_ref[...] + _dot(wd, wup_ref[...])
    logw = (-math.exp(-0.5)) * jax.nn.sigmoid(w_raw)
    a = jax.nn.sigmoid(a0_ref[...] + _dot(ad, aup_ref[...]))
    ones_bd = ones_ref[...]
    kk = k * kk_ref[...]
    kk = kk / jnp.maximum(jnp.sqrt(_head_sum(kk * kk, ones_bd)), 1e-12)
    ka = ka_ref[...]
    a_f, a_b = a[:, 0:w], a[:, w:2 * w]
    k_f = k * (1.0 + (a_f - 1.0) * ka)
    k_b = k * (1.0 + (a_b - 1.0) * ka)
    r_o[0] = r
    v_o[0] = v
    kk_o[0] = kk
    bonus_o[0] = _head_sum(r * (k_f + k_b) * rk_ref[...], ones_bd) * v
    lwf_o[0] = logw[:, 0:w]
    kf_o[0] = k_f
    bf_o[0] = kk * a_f
    lwb_o[0] = logw[:, w:2 * w]
    kb_o[0] = k_b
    bb_o[0] = kk * a_b


def _rwkv_prep(uf, mup, mun, w0, a0, wup, aup, k_k, k_a, r_k, ones_bd):
    b, t = uf.shape[0], uf.shape[1]
    tm = _tile(t, 256)
    nblk8 = t // SUBLANES
    step8 = tm // SUBLANES
    const = lambda bb, i: (0, 0)
    in_specs = [
        pl.BlockSpec((1, tm, CIN_PAD), lambda bb, i: (bb, i, 0)),
        pl.BlockSpec((1, SUBLANES, CIN_PAD), lambda bb, i: (bb, jnp.maximum(i * step8 - 1, 0), 0)),
        pl.BlockSpec((1, SUBLANES, CIN_PAD), lambda bb, i: (bb, jnp.minimum((i + 1) * step8, nblk8 - 1), 0)),
        pl.BlockSpec((1, CIN_PAD), const), pl.BlockSpec((1, CIN_PAD), const),
        pl.BlockSpec((1, 2 * C_WIDTH), const), pl.BlockSpec((1, 2 * C_WIDTH), const),
        pl.BlockSpec((LANES, 2 * C_WIDTH), const), pl.BlockSpec((LANES, 2 * C_WIDTH), const),
        pl.BlockSpec((1, C_WIDTH), const), pl.BlockSpec((1, C_WIDTH), const), pl.BlockSpec((1, C_WIDTH), const),
        pl.BlockSpec((2 * LANES, 2 * LANES), const),
    ]
    out_specs = [pl.BlockSpec((1, tm, C_WIDTH), lambda bb, i: (bb, i, 0))] * 10
    out_shape = [jax.ShapeDtypeStruct((b, t, C_WIDTH), F32)] * 10
    return pl.pallas_call(
        _rwkv_prep_kernel,
        grid=(b, t // tm),
        in_specs=in_specs, out_specs=out_specs, out_shape=out_shape,
        compiler_params=_cparams(("parallel", "parallel")),
        name="rwkv_prep",
    )(uf, uf, uf, mup, mun, w0, a0, wup, aup, k_k, k_a, r_k, ones_bd)


def _chunk_operands(r, v, kk, logw, kd, bd, rev):
    c = CHUNK
    rows = lax.broadcasted_iota(jnp.int32, (c, 1), 0)
    cum = logw
    shift = 1
    while shift < c:
        if rev:
            cum = cum + jnp.where(rows < c - shift, pltpu.roll(cum, c - shift, 0), 0.0)
        else:
            cum = cum + jnp.where(rows >= shift, pltpu.roll(cum, shift, 0), 0.0)
        shift *= 2
    tot = cum[0:1, :] if rev else cum[c - 1:c, :]
    e_neg = jnp.exp(-cum)
    e_rem = jnp.exp(tot - cum)
    return dict(a=-kk * jnp.exp(cum - logw), r=r * jnp.exp(cum), b=bd * e_neg, k=kd * e_neg,
                bc=bd * e_rem, kc=kd * e_rem, v=v, g_tot=jnp.exp(tot))


def _scan_chunk(dirs, h_scr, y_refs):
    c = CHUNK
    lane = lax.broadcasted_iota(jnp.int32, (1, LANES), 1)
    m_lo = jnp.where(lane < HEAD_DIM, 1.0, 0.0)
    m_hi = 1.0 - m_lo
    row = lax.broadcasted_iota(jnp.int32, (LANES, LANES), 0)
    col = lax.broadcasted_iota(jnp.int32, (LANES, LANES), 1)
    same = (row // c) == (col // c)
    diag = row == col
    eye = jnp.where(diag, 1.0, 0.0)

    chains = []
    for d_idx, (ops, rev) in enumerate(dirs):
        before = (col > row) if rev else (col < row)
        strict = same & before
        incl = same & (before | diag)
        for p in range(N_PAIRS):
            sl = slice(p * LANES, (p + 1) * LANES)

            def stack(z, sl=sl):
                zp = z[:, sl]
                return jnp.concatenate([zp * m_lo, zp * m_hi], axis=0)

            ch = {name: stack(ops[name]) for name in ("a", "r", "b", "k", "bc", "kc", "v")}
            ch.update(d=d_idx, sl=sl, strict=strict, incl=incl, g_tot=ops["g_tot"][:, sl])
            chains.append(ch)

    for ch in chains:
        ar = jnp.concatenate([ch["a"], ch["r"]], axis=0).astype(BF16)
        bk = jnp.concatenate([ch["b"], ch["k"]], axis=0).astype(BF16)
        ch["sc"] = _dot_nt(ar, bk)
    for ch in chains:
        sc = ch.pop("sc")
        ch["a_ab"] = jnp.where(ch["strict"], sc[0:LANES, 0:LANES], 0.0)
        a_ak = jnp.where(ch["strict"], sc[0:LANES, LANES:2 * LANES], 0.0)
        ch["m_rb"] = jnp.where(ch["incl"], sc[LANES:2 * LANES, 0:LANES], 0.0).astype(BF16)
        m_rk = jnp.where(ch["incl"], sc[LANES:2 * LANES, LANES:2 * LANES], 0.0)
        ch["vb"] = ch["v"].astype(BF16)
        xm = _dot(jnp.concatenate([a_ak, m_rk], axis=0).astype(BF16), ch["vb"])
        ch["x"] = xm[0:LANES]
        ch["mv"] = xm[LANES:2 * LANES]
        ch["kv"] = _dot_tn(ch["kc"].astype(BF16), ch["vb"])
        ch["t"] = eye + ch["a_ab"]
    for ch in chains:
        ab = ch.pop("a_ab").astype(BF16)
        ch["pw"] = _dot(ab, ab)
    n_levels = int(math.log2(c)) - 1
    for lvl in range(n_levels):
        last = lvl == n_levels - 1
        for ch in chains:
            pwb = ch["pw"].astype(BF16)
            tb = ch["t"].astype(BF16)
            if last:
                ch["t"] = ch["t"] + _dot(pwb, tb)
            else:
                nxt = _dot(pwb, jnp.concatenate([pwb, tb], axis=1))
                ch["pw"] = nxt[:, 0:LANES]
                ch["t"] = ch["t"] + nxt[:, LANES:2 * LANES]
    for ch in chains:
        ax = jnp.concatenate([ch["a"], ch["x"]], axis=1).astype(BF16)
        ch["g"] = _dot(ch["t"].astype(BF16), ax).astype(BF16)
    for ch in chains:
        ch["ry"] = jnp.concatenate([ch["r"], ch["mv"]], axis=1) + _dot(ch["m_rb"], ch["g"])
        ch["pq"] = _dot_tn(ch["bc"].astype(BF16), ch["g"])
    for ch in chains:
        q_mat = ch["pq"][:, LANES:2 * LANES] + ch["kv"]
        d_idx, p = ch["d"], ch["sl"].start // LANES
        h_old = h_scr[d_idx, p]
        rp = jnp.concatenate([ch["ry"][:, 0:LANES], ch["pq"][:, 0:LANES]], axis=0).astype(BF16)
        yh = _dot(rp, h_old.astype(BF16))
        y2 = ch["ry"][:, LANES:2 * LANES] + yh[0:LANES]
        g_rows = jnp.transpose(jnp.broadcast_to(ch["g_tot"], (LANES, LANES)))
        h_scr[d_idx, p] = g_rows * h_old + yh[LANES:2 * LANES] + q_mat
        y_refs[d_idx][0, :, ch["sl"]] = y2[0:c, :] + y2[c:2 * c, :]


def _rwkv_scan_kernel(rf, vf, kkf, lwf, kf, bf, rb, vb, kkb, lwb, kb, bb, h0f, h0b,
                      yf, yb, hff, hfb, h_scr):
    i = pl.program_id(1)

    @pl.when(i == 0)
    def _():
        h_scr[0] = h0f[0]
        h_scr[1] = h0b[0]

    fwd = _chunk_operands(rf[0], vf[0], kkf[0], lwf[0], kf[0], bf[0], rev=False)
    bwd = _chunk_operands(rb[0], vb[0], kkb[0], lwb[0], kb[0], bb[0], rev=True)
    _scan_chunk([(fwd, False), (bwd, True)], h_scr, (yf, yb))

    @pl.when(i == pl.num_programs(1) - 1)
    def _():
        hff[0] = h_scr[0]
        hfb[0] = h_scr[1]


def _rwkv_scan(prep, h0f, h0b):
    r, v, kk, _, lwf, kf, bf, lwb, kb, bb = prep
    b, t = r.shape[0], r.shape[1]
    nc = t // CHUNK
    fwd = pl.BlockSpec((1, CHUNK, C_WIDTH), lambda bb_, i: (bb_, i, 0))
    bwd = pl.BlockSpec((1, CHUNK, C_WIDTH), lambda bb_, i: (bb_, nc - 1 - i, 0))
    st = pl.BlockSpec((1, N_PAIRS, LANES, LANES), lambda bb_, i: (bb_, 0, 0, 0))
    return pl.pallas_call(
        _rwkv_scan_kernel,
        grid=(b, nc),
        in_specs=[fwd] * 6 + [bwd] * 6 + [st, st],
        out_specs=[fwd, bwd, st, st],
        out_shape=[jax.ShapeDtypeStruct((b, t, C_WIDTH), F32)] * 2
        + [jax.ShapeDtypeStruct((b, N_PAIRS, LANES, LANES), F32)] * 2,
        scratch_shapes=[pltpu.VMEM((2, N_PAIRS, LANES, LANES), F32)],
        compiler_params=_cparams(("parallel", "arbitrary")),
        name="rwkv_scan",
    )(r, v, kk, lwf, kf, bf, r, v, kk, lwb, kb, bb, h0f, h0b)


def _state_to_pairs(s):
    h = jnp.swapaxes(s.astype(F32), -1, -2).reshape(s.shape[0], N_PAIRS, 2, HEAD_DIM, HEAD_DIM)
    z = jnp.zeros_like(h[:, :, 0])
    top = jnp.concatenate([h[:, :, 0], z], axis=-1)
    bot = jnp.concatenate([z, h[:, :, 1]], axis=-1)
    return jnp.concatenate([top, bot], axis=-2)


def _pairs_to_state(hp):
    e = hp[:, :, 0:HEAD_DIM, 0:HEAD_DIM]
    o = hp[:, :, HEAD_DIM:, HEAD_DIM:]
    h = jnp.stack([e, o], axis=2).reshape(hp.shape[0], N_HEADS, HEAD_DIM, HEAD_DIM)
    return jnp.swapaxes(h, -1, -2)


def _merge_kernel(x_ref, gate_ref, ya_ref, yb_ref, yf_ref, ybw_ref, bonus_ref, cz_ref, g_ref,
                  woa_ref, wob_ref, woc_ref, wout_ref, lnw_ref, lnb_ref, ones_ref, o_ref):
    ones_bd = ones_ref[...]
    y = yf_ref[0] + ybw_ref[0]
    mu = _head_sum(y, ones_bd) * (1.0 / HEAD_DIM)
    yc = y - mu
    var = _head_sum(yc * yc, ones_bd) * (1.0 / HEAD_DIM)
    yn = yc * lax.rsqrt(var + C_GN_EPS) * lnw_ref[...] + lnb_ref[...] + bonus_ref[0]
    ycg = (yn * _silu(cz_ref[0].astype(F32))).astype(BF16)
    d = D_MODEL
    ga = jax.nn.sigmoid(g_ref[0, :, 0:d].astype(F32))
    gb = jax.nn.sigmoid(g_ref[0, :, d:2 * d].astype(F32))
    gc = jax.nn.sigmoid(g_ref[0, :, 2 * d:3 * d].astype(F32))
    mixed = (ga * _dot(ya_ref[0], woa_ref[...]) + gb * _dot(yb_ref[0], wob_ref[...])
             + gc * _dot(ycg, woc_ref[...]))
    out = _dot(mixed.astype(BF16), wout_ref[...])
    o_ref[0] = x_ref[0] + gate_ref[0] * out


def _merge(x, gate, ya, yb, yf, ybw, bonus, ub, woa, wob, woc, wout, lnw, lnb, ones_bd):
    nb, t, d = x.shape
    tm = _tile(t, 256)
    tok = lambda w: pl.BlockSpec((1, tm, w), lambda b, i: (b, i, 0))
    const = lambda b, i: (0, 0)
    in_specs = [tok(d), pl.BlockSpec((1, 1, d), lambda b, i: (b, 0, 0)),
                tok(C_WIDTH), tok(C_WIDTH), tok(C_WIDTH), tok(C_WIDTH), tok(C_WIDTH),
                pl.BlockSpec((1, tm, C_WIDTH), lambda b, i: (b, i, OFF_CZ // C_WIDTH)),
                pl.BlockSpec((1, tm, G_W), lambda b, i: (b, i, OFF_G // G_W)),
                pl.BlockSpec((C_WIDTH, d), const), pl.BlockSpec((C_WIDTH, d), const),
                pl.BlockSpec((C_WIDTH, d), const), pl.BlockSpec((d, d), const),
                pl.BlockSpec((1, C_WIDTH), const), pl.BlockSpec((1, C_WIDTH), const),
                pl.BlockSpec((2 * LANES, 2 * LANES), const)]
    return pl.pallas_call(
        _merge_kernel,
        grid=(nb, t // tm),
        in_specs=in_specs,
        out_specs=tok(d),
        out_shape=jax.ShapeDtypeStruct((nb, t, d), F32),
        compiler_params=_cparams(("parallel", "parallel")),
        name="merge",
    )(x, gate, ya, yb, yf, ybw, bonus, ub, ub, woa, wob, woc, wout, lnw, lnb, ones_bd)


def _final_norm_kernel(x_ref, w_ref, o_ref):
    x = x_ref[...]
    o_ref[...] = x * lax.rsqrt(jnp.mean(x * x, axis=-1, keepdims=True) + NORM_EPS) * w_ref[...]


def _final_norm(x, w):
    shape = x.shape
    x2 = x.reshape(-1, shape[-1])
    n, d = x2.shape
    tm = _tile(n, 1024)
    out = pl.pallas_call(
        _final_norm_kernel,
        grid=(n // tm,),
        in_specs=[pl.BlockSpec((tm, d), lambda i: (i, 0)), pl.BlockSpec((1, d), lambda i: (0, 0))],
        out_specs=pl.BlockSpec((tm, d), lambda i: (i, 0)),
        out_shape=jax.ShapeDtypeStruct((n, d), F32),
        compiler_params=_cparams(("parallel",)),
        name="final_norm",
    )(x2, w.reshape(1, d))
    return out.reshape(shape)


def _rope_tables(n_tokens):
    t = np.arange(n_tokens)
    pos = np.stack([t // GRID_W, t % GRID_W], axis=0).astype(np.float32)
    lane = np.arange(LANES)

    def build(active, part, freq_idx, half, first):
        inv = jnp.asarray(ROPE_THETA, F32) ** (-jnp.asarray(freq_idx, F32) / half)
        posm = jnp.asarray(pos)[jnp.asarray(part)]
        ang = posm.T * inv[None, :]
        cos = jnp.where(jnp.asarray(active)[None, :], jnp.cos(ang), 1.0)
        sin = jnp.where(jnp.asarray(active)[None, :], jnp.sin(ang), 0.0)
        s1 = jnp.where(jnp.asarray(first)[None, :], -sin, 0.0)
        s2 = jnp.where(jnp.asarray(first)[None, :], 0.0, sin)
        return cos.astype(F32), s1.astype(F32), s2.astype(F32)

    i = lane % HEAD_DIM
    tab_a = build(np.ones(LANES, bool), i // 32, (i % 32) % 16, 16, (i % 32) < 16)
    i = np.clip(lane - B_NOPE, 0, B_ROPE - 1)
    active = (lane >= B_NOPE) & (lane < B_NOPE + B_ROPE)
    tab_b = build(active, i // 16, (i % 16) % 8, 8, (i % 16) < 8)
    return tab_a, tab_b


def kernel(x_prompt, x_sample, cache_a_k, cache_a_v, cache_b_ckv, cache_b_krope, state_c_fwd, state_c_bwd,
           c, c_ctx, norm_w, w_mod, b_mod, w_in, a_qnorm_w, a_knorm_w, b_kvnorm_w, b_w_uk, b_w_uv,
           c_mu_prev, c_mu_next, c_w0, c_w_up, c_a0, c_a_up, c_k_k, c_k_a, c_r_k, c_lnx_w, c_lnx_b,
           w_oa, w_ob, w_oc, w_out, final_norm_w):
    depth = w_in.shape[0]
    bc, tc, d = x_prompt.shape
    bl, tl, _ = x_sample.shape
    past = cache_a_k.shape[2]
    assert d == D_MODEL and tc % CHUNK == 0 and tl % CHUNK == 0 and tl % GRID_W == 0

    f_idx, b_idx = _column_maps()
    w_all = _gather_cols(w_in, np.concatenate([f_idx, b_idx])).astype(BF16)
    uk_idx = np.full((WIDE,), -1, np.int64)
    for h in range(N_HEADS):
        uk_idx[h * LANES:h * LANES + B_NOPE] = h * B_NOPE + np.arange(B_NOPE)
    wuk = _gather_cols(b_w_uk, uk_idx).astype(BF16)
    wuv = b_w_uv.astype(BF16)
    knw = jnp.tile(a_knorm_w, (1, 2)).reshape(depth, 1, LANES)
    qnw = jnp.tile(a_qnorm_w, (1, 2)).reshape(depth, 1, LANES)
    cnw = b_kvnorm_w.reshape(depth, 1, LANES)
    pad_c = CIN_PAD - C_SHIFT_DIM
    mup = jnp.pad(c_mu_prev, ((0, 0), (0, pad_c))).reshape(depth, 1, CIN_PAD)
    mun = jnp.pad(c_mu_next, ((0, 0), (0, pad_c))).reshape(depth, 1, CIN_PAD)
    w0 = c_w0.reshape(depth, 1, 2 * C_WIDTH)
    a0 = c_a0.reshape(depth, 1, 2 * C_WIDTH)

    def lora_stack(up):
        z = jnp.zeros_like(up[:, 0])
        top = jnp.concatenate([up[:, 0], z], axis=-1)
        bot = jnp.concatenate([z, up[:, 1]], axis=-1)
        return jnp.concatenate([top, bot], axis=-2).astype(BF16)

    wup, aup = lora_stack(c_w_up), lora_stack(c_a_up)
    k_k = c_k_k.reshape(depth, 1, C_WIDTH)
    k_a = c_k_a.reshape(depth, 1, C_WIDTH)
    r_k = c_r_k.reshape(depth, 1, C_WIDTH)
    lnw = c_lnx_w.reshape(depth, 1, C_WIDTH)
    lnb = c_lnx_b.reshape(depth, 1, C_WIDTH)
    woa, wob, woc, wout = (w.astype(BF16) for w in (w_oa, w_ob, w_oc, w_out))
    hid = np.arange(2 * LANES) // HEAD_DIM
    ones_bd = jnp.asarray((hid[:, None] == hid[None, :]).astype(np.float32)).astype(BF16)
    tab_a, tab_b = _rope_tables(tl)

    rows = -(-(bl + 1) // SUBLANES) * SUBLANES
    cond = jnp.zeros((rows, d), F32).at[0:bl].set(c).at[bl].set(c_ctx)
    mod = _modulation(cond, w_mod, b_mod)

    def mod_parts(l, lo, hi):
        m = mod[l, lo:hi]
        return tuple(m[:, j * d:(j + 1) * d].reshape(hi - lo, 1, d) for j in range(3))

    zeros_state = jnp.zeros((bc, N_PAIRS, LANES, LANES), F32)
    cache_kr_pad = jnp.pad(cache_b_krope, ((0, 0), (0, 0), (0, 0), (B_NOPE, LANES - B_NOPE - B_ROPE)))
    cache_kv = jnp.concatenate([cache_a_k.reshape(bl, depth, past, LANES),
                                cache_a_v.reshape(bl, depth, past, LANES),
                                cache_b_ckv, cache_kr_pad], axis=-1)

    xp = x_prompt.reshape(1, bc * tc, d)
    xs = x_sample
    new_ak, new_av, new_ckv, new_kr, new_sf, new_sb = [], [], [], [], [], []

    def mixer_tail(x, gate, ub, uf, ka, va, kb, vb, tabs_q, h0f, h0b, l, nb_tok):
        b_, t_ = ub.shape[0], ub.shape[1]
        ya = _attn(ub, ka, va, tabs_q[0], qnw[l], mode="A", q_off=OFF_AQ, z_off=OFF_AZ)
        yb = _attn(ub, kb, vb, tabs_q[1], None, mode="B", q_off=OFF_BQ, z_off=OFF_BZ)
        prep = _rwkv_prep(uf, mup[l], mun[l], w0[l], a0[l], wup[l], aup[l], k_k[l], k_a[l], r_k[l], ones_bd)
        yf, ybw, hff, hfb = _rwkv_scan(prep, h0f, h0b)
        rs = lambda z: z.reshape(nb_tok, (b_ * t_) // nb_tok, z.shape[-1])
        x_new = _merge(x, gate, rs(ya), rs(yb), rs(yf), rs(ybw), rs(prep[3]), rs(ub),
                       woa[l], wob[l], woc[l], wout[l], lnw[l], lnb[l], ones_bd)
        return x_new, hff, hfb

    for l in range(depth):
        nw = norm_w[l].reshape(1, d)
        shift, scale, gate = mod_parts(l, bl, bl + 1)
        uf, ub = _proj(xp, shift, scale, nw, w_all[l])
        uf, ub = uf.reshape(bc, tc, NF), ub.reshape(bc, tc, NB16)
        ka, va, kb, vb, k_n, ckv_n = _kv_prep(uf, CIN_PAD // KV_W, None, None, knw[l], cnw[l], wuk[l], wuv[l],
                                               emit=True)
        xp, hff, hfb = mixer_tail(xp, gate, ub, uf, ka, va, kb, vb, (None, None), zeros_state, zeros_state, l, 1)
        new_ak.append(k_n.reshape(bc, tc, A_KV_HEADS, HEAD_DIM))
        new_av.append(uf[:, :, CIN_PAD + LANES:CIN_PAD + 2 * LANES].reshape(bc, tc, A_KV_HEADS, HEAD_DIM))
        new_ckv.append(ckv_n)
        new_kr.append(uf[:, :, CIN_PAD + 3 * LANES + B_NOPE:CIN_PAD + 3 * LANES + B_NOPE + B_ROPE])
        new_sf.append(_pairs_to_state(hff))
        new_sb.append(_pairs_to_state(hfb))
        shift, scale, gate = mod_parts(l, 0, bl)
        uf, ub = _proj(xs, shift, scale, nw, w_all[l])
        ka, va, kb, vb = _kv_prep(uf, CIN_PAD // KV_W, cache_kv[:, l], tab_a + tab_b,
                                  knw[l], cnw[l], wuk[l], wuv[l], emit=False)
        xs, _, _ = mixer_tail(xs, gate, ub, uf, ka, va, kb, vb, (tab_a, tab_b),
                              _state_to_pairs(state_c_fwd[:, l]), _state_to_pairs(state_c_bwd[:, l]), l, bl)

    y_prompt = _final_norm(xp, final_norm_w).reshape(bc, tc, d)
    y_sample = _final_norm(xs, final_norm_w)
    return (y_prompt, y_sample,
            jnp.stack(new_ak, axis=1), jnp.stack(new_av, axis=1),
            jnp.stack(new_ckv, axis=1), jnp.stack(new_kr, axis=1),
            jnp.stack(new_sf, axis=1), jnp.stack(new_sb, axis=1))
```

```python
import functools
import math

import numpy as np
import jax
import jax.numpy as jnp
from jax import lax
from jax.experimental import pallas as pl
from jax.experimental.pallas import tpu as pltpu

F32 = jnp.float32
BF16 = jnp.bfloat16

D_MODEL = 1024
GRID_W = 64
HEAD_DIM = 64
N_HEADS = 8
A_KV_HEADS = 2
A_GROUPS = N_HEADS // A_KV_HEADS
B_NOPE = 64
B_ROPE = 32
B_KV_RANK = 128
C_WIDTH = N_HEADS * HEAD_DIM
C_LORA = 64
C_SHIFT_DIM = 3 * C_WIDTH + 4 * C_LORA
ROPE_THETA = 10000.0
NORM_EPS = 1e-6
C_GN_EPS = 64e-5
LOG2_E = math.log2(math.e)

LANES = 128
SUBLANES = 8
VMEM_LIMIT_BYTES = 56 * 1024 * 1024

PROJ_COL_CHUNK = 512
ATTN_Q_TILE = 512
ATTN_SUB_TILE = 256
ATTN_KEY_CHUNK = 256
ATTN_SKEW = 2
VT_ROWS = HEAD_DIM + 16
CHUNK = 64
N_PAIRS = N_HEADS // 2
WIDE = N_HEADS * LANES

CIN_PAD = 2048
KV_W = 4 * LANES
NF = CIN_PAD + KV_W
G_W = 3 * D_MODEL
OFF_G = 0
OFF_AQ = OFF_G + G_W
OFF_AZ = OFF_AQ + WIDE
OFF_BQ = OFF_AZ + C_WIDTH
OFF_BZ = OFF_BQ + WIDE
OFF_CZ = OFF_BZ + C_WIDTH
NB16 = OFF_CZ + C_WIDTH

_R_AQ, _R_AK, _R_AV, _R_AZ = 0, 512, 640, 768
_R_BQ, _R_CKV, _R_KR, _R_BZ = 1280, 2048, 2176, 2208
_R_CIN, _R_CZ, _R_G = 2720, 4512, 5024


def _column_maps():
    f = np.full((NF,), -1, np.int64)
    f[0:C_SHIFT_DIM] = _R_CIN + np.arange(C_SHIFT_DIM)
    f[CIN_PAD:CIN_PAD + 128] = _R_AK + np.arange(128)
    f[CIN_PAD + 128:CIN_PAD + 256] = _R_AV + np.arange(128)
    f[CIN_PAD + 256:CIN_PAD + 384] = _R_CKV + np.arange(128)
    f[CIN_PAD + 384 + 64:CIN_PAD + 384 + 96] = _R_KR + np.arange(32)
    b = np.full((NB16,), -1, np.int64)
    b[OFF_G:OFF_G + G_W] = _R_G + np.arange(G_W)
    for h in range(N_HEADS):
        kv = h // A_GROUPS
        lo = OFF_AQ + h * LANES + kv * HEAD_DIM
        b[lo:lo + HEAD_DIM] = _R_AQ + h * HEAD_DIM + np.arange(HEAD_DIM)
        lo = OFF_BQ + h * LANES
        b[lo:lo + B_NOPE + B_ROPE] = _R_BQ + h * (B_NOPE + B_ROPE) + np.arange(B_NOPE + B_ROPE)
    b[OFF_AZ:OFF_AZ + C_WIDTH] = _R_AZ + np.arange(C_WIDTH)
    b[OFF_BZ:OFF_BZ + C_WIDTH] = _R_BZ + np.arange(C_WIDTH)
    b[OFF_CZ:OFF_CZ + C_WIDTH] = _R_CZ + np.arange(C_WIDTH)
    return f, b


def _gather_cols(w, idx):
    pieces, start = [], 0
    for pos in range(1, len(idx) + 1):
        run_ends = pos == len(idx) or (idx[pos] != idx[pos - 1] + 1 if idx[pos - 1] >= 0 else idx[pos] >= 0)
        if run_ends:
            if idx[start] >= 0:
                pieces.append(w[..., int(idx[start]):int(idx[pos - 1]) + 1])
            else:
                pieces.append(jnp.zeros(w.shape[:-1] + (pos - start,), w.dtype))
            start = pos
    return jnp.concatenate(pieces, axis=-1)


def _tile(n, pref):
    t = min(n, pref)
    assert n % t == 0, (n, pref)
    return t


def _cparams(sem):
    return pltpu.CompilerParams(dimension_semantics=sem, vmem_limit_bytes=VMEM_LIMIT_BYTES)


def _silu(z):
    return z * jax.nn.sigmoid(z)


def _dot(a, b):
    return jnp.dot(a, b, preferred_element_type=F32)


def _dot_nt(a, b):
    return lax.dot_general(a, b, (((1,), (1,)), ((), ())), preferred_element_type=F32)


def _dot_tn(a, b):
    return lax.dot_general(a, b, (((0,), (0,)), ((), ())), preferred_element_type=F32)


def _head_sum(x, ones_bd):
    hi = x.astype(BF16)
    lo = (x - hi.astype(F32)).astype(BF16)
    w = ones_bd.shape[0]
    halves = [_dot(hi[:, c:c + w], ones_bd) + _dot(lo[:, c:c + w], ones_bd) for c in range(0, x.shape[1], w)]
    return jnp.concatenate(halves, axis=1)


def _rope(x, cos, s1, s2, shift):
    n = x.shape[-1]
    return x * cos + pltpu.roll(x, n - shift, 1) * s1 + pltpu.roll(x, shift, 1) * s2


def _mod_kernel(cond_ref, w_ref, b_ref, o_ref):
    s = _silu(cond_ref[...]).astype(BF16)
    o_ref[0] = _dot(s, w_ref[0].astype(BF16)) + b_ref[0]


def _modulation(cond, w_mod, b_mod):
    depth, d, n = w_mod.shape
    rows = cond.shape[0]
    tn = _tile(n, 1024)
    return pl.pallas_call(
        _mod_kernel,
        grid=(depth, n // tn),
        in_specs=[pl.BlockSpec((rows, d), lambda l, j: (0, 0)),
                  pl.BlockSpec((1, d, tn), lambda l, j: (l, 0, j)),
                  pl.BlockSpec((1, 1, tn), lambda l, j: (l, 0, j))],
        out_specs=pl.BlockSpec((1, rows, tn), lambda l, j: (l, 0, j)),
        out_shape=jax.ShapeDtypeStruct((depth, rows, n), F32),
        compiler_params=_cparams(("parallel", "parallel")),
        name="modulation",
    )(cond, w_mod, b_mod.reshape(depth, 1, n))


def _proj_kernel(x_ref, sh_ref, sc_ref, nw_ref, w_ref, of_ref, ob_ref):
    x = x_ref[0]
    ms = jnp.mean(x * x, axis=-1, keepdims=True)
    y = x * lax.rsqrt(ms + NORM_EPS) * nw_ref[...]
    h = (y * (1.0 + sc_ref[0]) + sh_ref[0]).astype(BF16)
    for c0 in range(0, NF, PROJ_COL_CHUNK):
        of_ref[0, :, c0:c0 + PROJ_COL_CHUNK] = _dot(h, w_ref[:, c0:c0 + PROJ_COL_CHUNK])
    for c0 in range(0, NB16, PROJ_COL_CHUNK):
        ob_ref[0, :, c0:c0 + PROJ_COL_CHUNK] = _dot(h, w_ref[:, NF + c0:NF + c0 + PROJ_COL_CHUNK]).astype(BF16)


def _proj(x, shift, scale, norm_w, w):
    nb, t, d = x.shape
    tm = _tile(t, 256)
    return pl.pallas_call(
        _proj_kernel,
        grid=(nb, t // tm),
        in_specs=[pl.BlockSpec((1, tm, d), lambda b, i: (b, i, 0)),
                  pl.BlockSpec((1, 1, d), lambda b, i: (b, 0, 0)),
                  pl.BlockSpec((1, 1, d), lambda b, i: (b, 0, 0)),
                  pl.BlockSpec((1, d), lambda b, i: (0, 0)),
                  pl.BlockSpec((d, NF + NB16), lambda b, i: (0, 0), pipeline_mode=pl.Buffered(1))],
        out_specs=[pl.BlockSpec((1, tm, NF), lambda b, i: (b, i, 0)),
                   pl.BlockSpec((1, tm, NB16), lambda b, i: (b, i, 0))],
        out_shape=[jax.ShapeDtypeStruct((nb, t, NF), F32), jax.ShapeDtypeStruct((nb, t, NB16), BF16)],
        compiler_params=_cparams(("parallel", "parallel")),
        name="proj",
    )(x, shift, scale, norm_w, w)


def _kv_prep_tile(kv, tabs, knw, cnw, wuk, wuv, outs, cache_outs, *, norm):
    ka_ref, va_ref, kb_ref, vb_ref = outs
    ak = kv[:, 0:LANES]
    av = kv[:, LANES:2 * LANES]
    ckv = kv[:, 2 * LANES:3 * LANES]
    kr = kv[:, 3 * LANES:4 * LANES]
    lo = lax.broadcasted_iota(jnp.int32, (1, LANES), 1) < HEAD_DIM
    if norm:
        sq = ak * ak
        s0 = jnp.sum(jnp.where(lo, sq, 0.0), axis=-1, keepdims=True)
        s1 = jnp.sum(jnp.where(lo, 0.0, sq), axis=-1, keepdims=True)
        ms = jnp.where(lo, s0, s1) * (1.0 / HEAD_DIM)
        ak = ak * lax.rsqrt(ms + NORM_EPS) * knw
        ckv = ckv * lax.rsqrt(jnp.mean(ckv * ckv, axis=-1, keepdims=True) + NORM_EPS) * cnw
    if cache_outs is not None:
        cache_outs[0][0] = ak
        cache_outs[1][0] = ckv
    if tabs is not None:
        ca, sa1, sa2, cb, sb1, sb2 = tabs
        ak = _rope(ak, ca, sa1, sa2, 16)
        kr = _rope(kr, cb, sb1, sb2, 8)
    ka_ref[0] = ak.astype(BF16)
    tm = kv.shape[0]
    ones_rows = jnp.ones((VT_ROWS - HEAD_DIM, tm), BF16)

    def put_pair(ref, h0, pair):
        pt = jnp.transpose(pair).astype(BF16)
        for e in range(2):
            ref[0, h0 + e, 0:HEAD_DIM, :] = pt[e * HEAD_DIM:(e + 1) * HEAD_DIM]
            ref[0, h0 + e, HEAD_DIM:VT_ROWS, :] = ones_rows

    put_pair(va_ref, 0, av)
    cb16 = ckv.astype(BF16)
    kn = _dot(cb16, wuk)
    for h in range(N_HEADS):
        kb_ref[0, :, h * LANES:(h + 1) * LANES] = (kn[:, h * LANES:(h + 1) * LANES] + kr).astype(BF16)
    vn = _dot(cb16, wuv)
    for p in range(N_PAIRS):
        put_pair(vb_ref, 2 * p, vn[:, p * LANES:(p + 1) * LANES])


def _kv_prep_kernel(*refs, rope, emit, n_new):
    it = iter(refs)
    kv_ref = next(it)
    cache_ref = next(it) if n_new is not None else None
    tabs = tuple(next(it)[...] for _ in range(6)) if rope else None
    knw, cnw, wuk, wuv = (next(it)[...] for _ in range(4))
    outs = (next(it), next(it), next(it), next(it))
    cache_outs = (next(it), next(it)) if emit else None
    if n_new is None:
        _kv_prep_tile(kv_ref[0], tabs, knw, cnw, wuk, wuv, outs, cache_outs, norm=True)
        return
    i = pl.program_id(1)

    @pl.when(i < n_new)
    def _():
        _kv_prep_tile(kv_ref[0], tabs, knw, cnw, wuk, wuv, outs, cache_outs, norm=True)

    @pl.when(i >= n_new)
    def _():
        _kv_prep_tile(cache_ref[0], None, knw, cnw, wuk, wuv, outs, None, norm=False)


def _kv_prep(kv, col_block, cache, tabs, knw, cnw, wuk, wuv, *, emit):
    b, t = kv.shape[0], kv.shape[1]
    rope = tabs is not None
    if cache is None:
        tm, n_new, total = _tile(t, 512), None, t
        in_specs = [pl.BlockSpec((1, tm, KV_W), lambda bb, i: (bb, i, col_block))]
        args = [kv]
        tab_spec = pl.BlockSpec((tm, LANES), lambda bb, i: (i, 0))
    else:
        past = cache.shape[1]
        tm = _tile(past, 512)
        assert t % tm == 0
        n_new, total = t // tm, t + past
        in_specs = [pl.BlockSpec((1, tm, KV_W), lambda bb, i: (bb, jnp.minimum(i, n_new - 1), col_block)),
                    pl.BlockSpec((1, tm, KV_W), lambda bb, i: (bb, jnp.maximum(i - n_new, 0), 0))]
        args = [kv, cache]
        tab_spec = pl.BlockSpec((tm, LANES), lambda bb, i: (jnp.minimum(i, n_new - 1), 0))
    if rope:
        in_specs += [tab_spec] * 6
        args += list(tabs)
    const = lambda bb, i: (0, 0)
    in_specs += [pl.BlockSpec((1, LANES), const), pl.BlockSpec((1, LANES), const),
                 pl.BlockSpec((B_KV_RANK, WIDE), const), pl.BlockSpec((B_KV_RANK, C_WIDTH), const)]
    args += [knw, cnw, wuk, wuv]
    def rows_out(w, dt):
        return pl.BlockSpec((1, tm, w), lambda bb, i: (bb, i, 0)), jax.ShapeDtypeStruct((b, total, w), dt)

    def vt_out(nh):
        return (pl.BlockSpec((1, nh, VT_ROWS, tm), lambda bb, i: (bb, 0, 0, i)),
                jax.ShapeDtypeStruct((b, nh, VT_ROWS, total), BF16))

    outs = [rows_out(LANES, BF16), vt_out(A_KV_HEADS), rows_out(WIDE, BF16), vt_out(N_HEADS)]
    if emit:
        outs += [rows_out(LANES, F32), rows_out(LANES, F32)]
    out_specs = [o[0] for o in outs]
    out_shape = [o[1] for o in outs]
    return pl.pallas_call(
        functools.partial(_kv_prep_kernel, rope=rope, emit=emit, n_new=n_new),
        grid=(b, total // tm),
        in_specs=in_specs, out_specs=out_specs, out_shape=out_shape,
        compiler_params=_cparams(("parallel", "parallel")),
        name="kv_prep",
    )(*args)


def _attn_kernel(*refs, mode, rope, scale):
    it = iter(refs)
    q_ref, z_ref, k_ref, vt_ref = next(it), next(it), next(it), next(it)
    if rope:
        cos, s1, s2 = (next(it)[...] for _ in range(3))
    if mode == "A":
        qnw = next(it)[...]
    o_ref = next(it)

    tq = q_ref.shape[1]
    tw = min(tq, ATTN_SUB_TILE)
    s_len = k_ref.shape[1]
    ck = ATTN_KEY_CHUNK if s_len % ATTN_KEY_CHUNK == 0 else LANES
    n_chunks = s_len // ck
    streams = [(e, r) for r in range(tq // tw) for e in range(2)]
    qs = {}
    for e, r in streams:
        rs = slice(r * tw, (r + 1) * tw)
        qh = q_ref[0, rs, e * LANES:(e + 1) * LANES].astype(F32)
        if mode == "A":
            ms = jnp.sum(qh * qh, axis=-1, keepdims=True) * (1.0 / HEAD_DIM)
            qh = qh * lax.rsqrt(ms + NORM_EPS) * qnw
        if rope:
            qh = _rope(qh, cos[rs], s1[rs], s2[rs], 16 if mode == "A" else 8)
        qs[e, r] = (qh * (scale * LOG2_E)).astype(BF16)

    m = {c: jnp.full((1, tw), -1e30, F32) for c in streams}
    acc = {c: jnp.zeros((VT_ROWS, tw), F32) for c in streams}
    st = {}

    def scores(c, t):
        e = c[0]
        kh = k_ref[0, t * ck:(t + 1) * ck, :] if mode == "A" else k_ref[0, t * ck:(t + 1) * ck, e * LANES:(e + 1) * LANES]
        st[c, t] = _dot_nt(kh, qs[c])

    def consume(c, j):
        s = st.pop((c, j))
        m_new = jnp.maximum(m[c], jnp.max(s, axis=0, keepdims=True))
        p = jnp.exp2(s - m_new).astype(BF16)
        vt = vt_ref[0, 0 if mode == "A" else c[0], :, j * ck:(j + 1) * ck]
        acc[c] = acc[c] * jnp.exp2(m[c] - m_new) + _dot(vt, p)
        m[c] = m_new

    for t in range(n_chunks + ATTN_SKEW):
        if t < n_chunks:
            for c in streams:
                scores(c, t)
        if t >= ATTN_SKEW:
            for c in streams:
                consume(c, t - ATTN_SKEW)
    for r in range(tq // tw):
        rs = slice(r * tw, (r + 1) * tw)
        o_t = jnp.concatenate([acc[e, r][0:HEAD_DIM] / acc[e, r][HEAD_DIM:HEAD_DIM + 1] for e in range(2)], axis=0)
        o_ref[0, rs, :] = (jnp.transpose(o_t) * _silu(z_ref[0, rs, :].astype(F32))).astype(o_ref.dtype)


def _attn(ub, k, v, tabs, qnw, *, mode, q_off, z_off):
    b, tq_all = ub.shape[0], ub.shape[1]
    s_len = k.shape[1]
    tq = _tile(tq_all, ATTN_Q_TILE)
    rope = tabs is not None
    qb0, zb0 = q_off // (2 * LANES), z_off // LANES
    in_specs = [pl.BlockSpec((1, tq, 2 * LANES), lambda bb, p, i: (bb, i, qb0 + p)),
                pl.BlockSpec((1, tq, LANES), lambda bb, p, i: (bb, i, zb0 + p))]
    if mode == "A":
        in_specs += [pl.BlockSpec((1, s_len, LANES), lambda bb, p, i: (bb, 0, 0)),
                     pl.BlockSpec((1, 1, VT_ROWS, s_len), lambda bb, p, i: (bb, p // 2, 0, 0))]
        scale = HEAD_DIM ** -0.5
    else:
        in_specs += [pl.BlockSpec((1, s_len, 2 * LANES), lambda bb, p, i: (bb, 0, p)),
                     pl.BlockSpec((1, 2, VT_ROWS, s_len), lambda bb, p, i: (bb, p, 0, 0))]
        scale = (B_NOPE + B_ROPE) ** -0.5
    args = [ub, ub, k, v]
    if rope:
        in_specs += [pl.BlockSpec((tq, LANES), lambda bb, p, i: (i, 0))] * 3
        args += list(tabs)
    if mode == "A":
        in_specs += [pl.BlockSpec((1, LANES), lambda bb, p, i: (0, 0))]
        args += [qnw]
    return pl.pallas_call(
        functools.partial(_attn_kernel, mode=mode, rope=rope, scale=scale),
        grid=(b, N_PAIRS, tq_all // tq),
        in_specs=in_specs,
        out_specs=pl.BlockSpec((1, tq, LANES), lambda bb, p, i: (bb, i, p)),
        out_shape=jax.ShapeDtypeStruct((b, tq_all, C_WIDTH), BF16),
        compiler_params=_cparams(("parallel", "parallel", "parallel")),
        name="attn_" + mode,
    )(*args)


def _rwkv_prep_kernel(cin_ref, hp_ref, hn_ref, mup_ref, mun_ref, w0_ref, a0_ref, wup_ref, aup_ref,
                      kk_ref, ka_ref, rk_ref, ones_ref,
                      r_o, v_o, kk_o, bonus_o, lwf_o, kf_o, bf_o, lwb_o, kb_o, bb_o):
    i = pl.program_id(1)
    n = pl.num_programs(1)
    s = cin_ref[0]
    tm = s.shape[0]
    prev_row = jnp.where(i > 0, hp_ref[0, SUBLANES - 1:SUBLANES, :], 0.0)
    next_row = jnp.where(i < n - 1, hn_ref[0, 0:1, :], 0.0)
    rows = lax.broadcasted_iota(jnp.int32, (tm, 1), 0)
    prev = jnp.where(rows == 0, prev_row, pltpu.roll(s, 1, 0))
    nxt = jnp.where(rows == tm - 1, next_row, pltpu.roll(s, tm - 1, 0))
    x = s + mup_ref[...] * (prev - s) + mun_ref[...] * (nxt - s)

    w = C_WIDTH
    r = x[:, 0:w]
    k = x[:, w:2 * w]
    v = x[:, 2 * w:3 * w]
    wd = jnp.tanh(x[:, 3 * w:3 * w + LANES]).astype(BF16)
    ad = x[:, 3 * w + LANES:3 * w + 2 * LANES].astype(BF16)
    w_raw = w0_ref[...] + _dot(wd, wup_ref[...])
    logw = (-math.exp(-0.5)) * jax.nn.sigmoid(w_raw)
    a = jax.nn.sigmoid(a0_ref[...] + _dot(ad, aup_ref[...]))
    ones_bd = ones_ref[...]
    kk = k * kk_ref[...]
    kk = kk / jnp.maximum(jnp.sqrt(_head_sum(kk * kk, ones_bd)), 1e-12)
    ka = ka_ref[...]
    a_f, a_b = a[:, 0:w], a[:, w:2 * w]
    k_f = k * (1.0 + (a_f - 1.0) * ka)
    k_b = k * (1.0 + (a_b - 1.0) * ka)
    r_o[0] = r
    v_o[0] = v
    kk_o[0] = kk
    bonus_o[0] = _head_sum(r * (k_f + k_b) * rk_ref[...], ones_bd) * v
    lwf_o[0] = logw[:, 0:w]
    kf_o[0] = k_f
    bf_o[0] = kk * a_f
    lwb_o[0] = logw[:, w:2 * w]
    kb_o[0] = k_b
    bb_o[0] = kk * a_b


def _rwkv_prep(uf, mup, mun, w0, a0, wup, aup, k_k, k_a, r_k, ones_bd):
    b, t = uf.shape[0], uf.shape[1]
    tm = _tile(t, 256)
    nblk8 = t // SUBLANES
    step8 = tm // SUBLANES
    const = lambda bb, i: (0, 0)
    in_specs = [
        pl.BlockSpec((1, tm, CIN_PAD), lambda bb, i: (bb, i, 0)),
        pl.BlockSpec((1, SUBLANES, CIN_PAD), lambda bb, i: (bb, jnp.maximum(i * step8 - 1, 0), 0)),
        pl.BlockSpec((1, SUBLANES, CIN_PAD), lambda bb, i: (bb, jnp.minimum((i + 1) * step8, nblk8 - 1), 0)),
        pl.BlockSpec((1, CIN_PAD), const), pl.BlockSpec((1, CIN_PAD), const),
        pl.BlockSpec((1, 2 * C_WIDTH), const), pl.BlockSpec((1, 2 * C_WIDTH), const),
        pl.BlockSpec((LANES, 2 * C_WIDTH), const), pl.BlockSpec((LANES, 2 * C_WIDTH), const),
        pl.BlockSpec((1, C_WIDTH), const), pl.BlockSpec((1, C_WIDTH), const), pl.BlockSpec((1, C_WIDTH), const),
        pl.BlockSpec((2 * LANES, 2 * LANES), const),
    ]
    out_specs = [pl.BlockSpec((1, tm, C_WIDTH), lambda bb, i: (bb, i, 0))] * 10
    out_shape = [jax.ShapeDtypeStruct((b, t, C_WIDTH), F32)] * 10
    return pl.pallas_call(
        _rwkv_prep_kernel,
        grid=(b, t // tm),
        in_specs=in_specs, out_specs=out_specs, out_shape=out_shape,
        compiler_params=_cparams(("parallel", "parallel")),
        name="rwkv_prep",
    )(uf, uf, uf, mup, mun, w0, a0, wup, aup, k_k, k_a, r_k, ones_bd)


def _chunk_operands(r, v, kk, logw, kd, bd, rev):
    c = CHUNK
    rows = lax.broadcasted_iota(jnp.int32, (c, 1), 0)
    cum = logw
    shift = 1
    while shift < c:
        if rev:
            cum = cum + jnp.where(rows < c - shift, pltpu.roll(cum, c - shift, 0), 0.0)
        else:
            cum = cum + jnp.where(rows >= shift, pltpu.roll(cum, shift, 0), 0.0)
        shift *= 2
    tot = cum[0:1, :] if rev else cum[c - 1:c, :]
    e_neg = jnp.exp(-cum)
    e_rem = jnp.exp(tot - cum)
    return dict(a=-kk * jnp.exp(cum - logw), r=r * jnp.exp(cum), b=bd * e_neg, k=kd * e_neg,
                bc=bd * e_rem, kc=kd * e_rem, v=v, g_tot=jnp.exp(tot))


def _scan_chunk(dirs, h_scr, y_refs):
    c = CHUNK
    lane = lax.broadcasted_iota(jnp.int32, (1, LANES), 1)
    m_lo = jnp.where(lane < HEAD_DIM, 1.0, 0.0)
    m_hi = 1.0 - m_lo
    row = lax.broadcasted_iota(jnp.int32, (LANES, LANES), 0)
    col = lax.broadcasted_iota(jnp.int32, (LANES, LANES), 1)
    same = (row // c) == (col // c)
    diag = row == col
    eye = jnp.where(diag, 1.0, 0.0)

    chains = []
    for d_idx, (ops, rev) in enumerate(dirs):
        before = (col > row) if rev else (col < row)
        strict = same & before
        incl = same & (before | diag)
        for p in range(N_PAIRS):
            sl = slice(p * LANES, (p + 1) * LANES)

            def stack(z, sl=sl):
                zp = z[:, sl]
                return jnp.concatenate([zp * m_lo, zp * m_hi], axis=0)

            ch = {name: stack(ops[name]) for name in ("a", "r", "b", "k", "bc", "kc", "v")}
            ch.update(d=d_idx, sl=sl, strict=strict, incl=incl, g_tot=ops["g_tot"][:, sl])
            chains.append(ch)

    for ch in chains:
        ar = jnp.concatenate([ch["a"], ch["r"]], axis=0).astype(BF16)
        bk = jnp.concatenate([ch["b"], ch["k"]], axis=0).astype(BF16)
        ch["sc"] = _dot_nt(ar, bk)
    for ch in chains:
        sc = ch.pop("sc")
        ch["a_ab"] = jnp.where(ch["strict"], sc[0:LANES, 0:LANES], 0.0)
        a_ak = jnp.where(ch["strict"], sc[0:LANES, LANES:2 * LANES], 0.0)
        ch["m_rb"] = jnp.where(ch["incl"], sc[LANES:2 * LANES, 0:LANES], 0.0).astype(BF16)
        m_rk = jnp.where(ch["incl"], sc[LANES:2 * LANES, LANES:2 * LANES], 0.0)
        ch["vb"] = ch["v"].astype(BF16)
        xm = _dot(jnp.concatenate([a_ak, m_rk], axis=0).astype(BF16), ch["vb"])
        ch["x"] = xm[0:LANES]
        ch["mv"] = xm[LANES:2 * LANES]
        ch["kv"] = _dot_tn(ch["kc"].astype(BF16), ch["vb"])
        ch["t"] = eye + ch["a_ab"]
    for ch in chains:
        ab = ch.pop("a_ab").astype(BF16)
        ch["pw"] = _dot(ab, ab)
    n_levels = int(math.log2(c)) - 1
    for lvl in range(n_levels):
        last = lvl == n_levels - 1
        for ch in chains:
            pwb = ch["pw"].astype(BF16)
            tb = ch["t"].astype(BF16)
            if last:
                ch["t"] = ch["t"] + _dot(pwb, tb)
            else:
                nxt = _dot(pwb, jnp.concatenate([pwb, tb], axis=1))
                ch["pw"] = nxt[:, 0:LANES]
                ch["t"] = ch["t"] + nxt[:, LANES:2 * LANES]
    for ch in chains:
        ax = jnp.concatenate([ch["a"], ch["x"]], axis=1).astype(BF16)
        ch["g"] = _dot(ch["t"].astype(BF16), ax).astype(BF16)
    for ch in chains:
        ch["ry"] = jnp.concatenate([ch["r"], ch["mv"]], axis=1) + _dot(ch["m_rb"], ch["g"])
        ch["pq"] = _dot_tn(ch["bc"].astype(BF16), ch["g"])
    for ch in chains:
        q_mat = ch["pq"][:, LANES:2 * LANES] + ch["kv"]
        d_idx, p = ch["d"], ch["sl"].start // LANES
        h_old = h_scr[d_idx, p]
        rp = jnp.concatenate([ch["ry"][:, 0:LANES], ch["pq"][:, 0:LANES]], axis=0).astype(BF16)
        yh = _dot(rp, h_old.astype(BF16))
        y2 = ch["ry"][:, LANES:2 * LANES] + yh[0:LANES]
        g_rows = jnp.transpose(jnp.broadcast_to(ch["g_tot"], (LANES, LANES)))
        h_scr[d_idx, p] = g_rows * h_old + yh[LANES:2 * LANES] + q_mat
        y_refs[d_idx][0, :, ch["sl"]] = y2[0:c, :] + y2[c:2 * c, :]


def _rwkv_scan_kernel(rf, vf, kkf, lwf, kf, bf, rb, vb, kkb, lwb, kb, bb, h0f, h0b,
                      yf, yb, hff, hfb, h_scr):
    i = pl.program_id(1)

    @pl.when(i == 0)
    def _():
        h_scr[0] = h0f[0]
        h_scr[1] = h0b[0]

    fwd = _chunk_operands(rf[0], vf[0], kkf[0], lwf[0], kf[0], bf[0], rev=False)
    bwd = _chunk_operands(rb[0], vb[0], kkb[0], lwb[0], kb[0], bb[0], rev=True)
    _scan_chunk([(fwd, False), (bwd, True)], h_scr, (yf, yb))

    @pl.when(i == pl.num_programs(1) - 1)
    def _():
        hff[0] = h_scr[0]
        hfb[0] = h_scr[1]


def _rwkv_scan(prep, h0f, h0b):
    r, v, kk, _, lwf, kf, bf, lwb, kb, bb = prep
    b, t = r.shape[0], r.shape[1]
    nc = t // CHUNK
    fwd = pl.BlockSpec((1, CHUNK, C_WIDTH), lambda bb_, i: (bb_, i, 0))
    bwd = pl.BlockSpec((1, CHUNK, C_WIDTH), lambda bb_, i: (bb_, nc - 1 - i, 0))
    st = pl.BlockSpec((1, N_PAIRS, LANES, LANES), lambda bb_, i: (bb_, 0, 0, 0))
    return pl.pallas_call(
        _rwkv_scan_kernel,
        grid=(b, nc),
        in_specs=[fwd] * 6 + [bwd] * 6 + [st, st],
        out_specs=[fwd, bwd, st, st],
        out_shape=[jax.ShapeDtypeStruct((b, t, C_WIDTH), F32)] * 2
        + [jax.ShapeDtypeStruct((b, N_PAIRS, LANES, LANES), F32)] * 2,
        scratch_shapes=[pltpu.VMEM((2, N_PAIRS, LANES, LANES), F32)],
        compiler_params=_cparams(("parallel", "arbitrary")),
        name="rwkv_scan",
    )(r, v, kk, lwf, kf, bf, r, v, kk, lwb, kb, bb, h0f, h0b)


def _state_to_pairs(s):
    h = jnp.swapaxes(s.astype(F32), -1, -2).reshape(s.shape[0], N_PAIRS, 2, HEAD_DIM, HEAD_DIM)
    z = jnp.zeros_like(h[:, :, 0])
    top = jnp.concatenate([h[:, :, 0], z], axis=-1)
    bot = jnp.concatenate([z, h[:, :, 1]], axis=-1)
    return jnp.concatenate([top, bot], axis=-2)


def _pairs_to_state(hp):
    e = hp[:, :, 0:HEAD_DIM, 0:HEAD_DIM]
    o = hp[:, :, HEAD_DIM:, HEAD_DIM:]
    h = jnp.stack([e, o], axis=2).reshape(hp.shape[0], N_HEADS, HEAD_DIM, HEAD_DIM)
    return jnp.swapaxes(h, -1, -2)


def _merge_kernel(x_ref, gate_ref, ya_ref, yb_ref, yf_ref, ybw_ref, bonus_ref, cz_ref, g_ref,
                  woa_ref, wob_ref, woc_ref, wout_ref, lnw_ref, lnb_ref, ones_ref, o_ref):
    ones_bd = ones_ref[...]
    y = yf_ref[0] + ybw_ref[0]
    mu = _head_sum(y, ones_bd) * (1.0 / HEAD_DIM)
    yc = y - mu
    var = _head_sum(yc * yc, ones_bd) * (1.0 / HEAD_DIM)
    yn = yc * lax.rsqrt(var + C_GN_EPS) * lnw_ref[...] + lnb_ref[...] + bonus_ref[0]
    ycg = (yn * _silu(cz_ref[0].astype(F32))).astype(BF16)
    d = D_MODEL
    ga = jax.nn.sigmoid(g_ref[0, :, 0:d].astype(F32))
    gb = jax.nn.sigmoid(g_ref[0, :, d:2 * d].astype(F32))
    gc = jax.nn.sigmoid(g_ref[0, :, 2 * d:3 * d].astype(F32))
    mixed = (ga * _dot(ya_ref[0], woa_ref[...]) + gb * _dot(yb_ref[0], wob_ref[...])
             + gc * _dot(ycg, woc_ref[...]))
    out = _dot(mixed.astype(BF16), wout_ref[...])
    o_ref[0] = x_ref[0] + gate_ref[0] * out


def _merge(x, gate, ya, yb, yf, ybw, bonus, ub, woa, wob, woc, wout, lnw, lnb, ones_bd):
    nb, t, d = x.shape
    tm = _tile(t, 256)
    tok = lambda w: pl.BlockSpec((1, tm, w), lambda b, i: (b, i, 0))
    const = lambda b, i: (0, 0)
    in_specs = [tok(d), pl.BlockSpec((1, 1, d), lambda b, i: (b, 0, 0)),
                tok(C_WIDTH), tok(C_WIDTH), tok(C_WIDTH), tok(C_WIDTH), tok(C_WIDTH),
                pl.BlockSpec((1, tm, C_WIDTH), lambda b, i: (b, i, OFF_CZ // C_WIDTH)),
                pl.BlockSpec((1, tm, G_W), lambda b, i: (b, i, OFF_G // G_W)),
                pl.BlockSpec((C_WIDTH, d), const), pl.BlockSpec((C_WIDTH, d), const),
                pl.BlockSpec((C_WIDTH, d), const), pl.BlockSpec((d, d), const),
                pl.BlockSpec((1, C_WIDTH), const), pl.BlockSpec((1, C_WIDTH), const),
                pl.BlockSpec((2 * LANES, 2 * LANES), const)]
    return pl.pallas_call(
        _merge_kernel,
        grid=(nb, t // tm),
        in_specs=in_specs,
        out_specs=tok(d),
        out_shape=jax.ShapeDtypeStruct((nb, t, d), F32),
        compiler_params=_cparams(("parallel", "parallel")),
        name="merge",
    )(x, gate, ya, yb, yf, ybw, bonus, ub, ub, woa, wob, woc, wout, lnw, lnb, ones_bd)


def _final_norm_kernel(x_ref, w_ref, o_ref):
    x = x_ref[...]
    o_ref[...] = x * lax.rsqrt(jnp.mean(x * x, axis=-1, keepdims=True) + NORM_EPS) * w_ref[...]


def _final_norm(x, w):
    shape = x.shape
    x2 = x.reshape(-1, shape[-1])
    n, d = x2.shape
    tm = _tile(n, 1024)
    out = pl.pallas_call(
        _final_norm_kernel,
        grid=(n // tm,),
        in_specs=[pl.BlockSpec((tm, d), lambda i: (i, 0)), pl.BlockSpec((1, d), lambda i: (0, 0))],
        out_specs=pl.BlockSpec((tm, d), lambda i: (i, 0)),
        out_shape=jax.ShapeDtypeStruct((n, d), F32),
        compiler_params=_cparams(("parallel",)),
        name="final_norm",
    )(x2, w.reshape(1, d))
    return out.reshape(shape)


def _rope_tables(n_tokens):
    t = np.arange(n_tokens)
    pos = np.stack([t // GRID_W, t % GRID_W], axis=0).astype(np.float32)
    lane = np.arange(LANES)

    def build(active, part, freq_idx, half, first):
        inv = jnp.asarray(ROPE_THETA, F32) ** (-jnp.asarray(freq_idx, F32) / half)
        posm = jnp.asarray(pos)[jnp.asarray(part)]
        ang = posm.T * inv[None, :]
        cos = jnp.where(jnp.asarray(active)[None, :], jnp.cos(ang), 1.0)
        sin = jnp.where(jnp.asarray(active)[None, :], jnp.sin(ang), 0.0)
        s1 = jnp.where(jnp.asarray(first)[None, :], -sin, 0.0)
        s2 = jnp.where(jnp.asarray(first)[None, :], 0.0, sin)
        return cos.astype(F32), s1.astype(F32), s2.astype(F32)

    i = lane % HEAD_DIM
    tab_a = build(np.ones(LANES, bool), i // 32, (i % 32) % 16, 16, (i % 32) < 16)
    i = np.clip(lane - B_NOPE, 0, B_ROPE - 1)
    active = (lane >= B_NOPE) & (lane < B_NOPE + B_ROPE)
    tab_b = build(active, i // 16, (i % 16) % 8, 8, (i % 16) < 8)
    return tab_a, tab_b


def kernel(x_prompt, x_sample, cache_a_k, cache_a_v, cache_b_ckv, cache_b_krope, state_c_fwd, state_c_bwd,
           c, c_ctx, norm_w, w_mod, b_mod, w_in, a_qnorm_w, a_knorm_w, b_kvnorm_w, b_w_uk, b_w_uv,
           c_mu_prev, c_mu_next, c_w0, c_w_up, c_a0, c_a_up, c_k_k, c_k_a, c_r_k, c_lnx_w, c_lnx_b,
           w_oa, w_ob, w_oc, w_out, final_norm_w):
    depth = w_in.shape[0]
    bc, tc, d = x_prompt.shape
    bl, tl, _ = x_sample.shape
    past = cache_a_k.shape[2]
    assert d == D_MODEL and tc % CHUNK == 0 and tl % CHUNK == 0 and tl % GRID_W == 0

    f_idx, b_idx = _column_maps()
    w_all = _gather_cols(w_in, np.concatenate([f_idx, b_idx])).astype(BF16)
    uk_idx = np.full((WIDE,), -1, np.int64)
    for h in range(N_HEADS):
        uk_idx[h * LANES:h * LANES + B_NOPE] = h * B_NOPE + np.arange(B_NOPE)
    wuk = _gather_cols(b_w_uk, uk_idx).astype(BF16)
    wuv = b_w_uv.astype(BF16)
    knw = jnp.tile(a_knorm_w, (1, 2)).reshape(depth, 1, LANES)
    qnw = jnp.tile(a_qnorm_w, (1, 2)).reshape(depth, 1, LANES)
    cnw = b_kvnorm_w.reshape(depth, 1, LANES)
    pad_c = CIN_PAD - C_SHIFT_DIM
    mup = jnp.pad(c_mu_prev, ((0, 0), (0, pad_c))).reshape(depth, 1, CIN_PAD)
    mun = jnp.pad(c_mu_next, ((0, 0), (0, pad_c))).reshape(depth, 1, CIN_PAD)
    w0 = c_w0.reshape(depth, 1, 2 * C_WIDTH)
    a0 = c_a0.reshape(depth, 1, 2 * C_WIDTH)

    def lora_stack(up):
        z = jnp.zeros_like(up[:, 0])
        top = jnp.concatenate([up[:, 0], z], axis=-1)
        bot = jnp.concatenate([z, up[:, 1]], axis=-1)
        return jnp.concatenate([top, bot], axis=-2).astype(BF16)

    wup, aup = lora_stack(c_w_up), lora_stack(c_a_up)
    k_k = c_k_k.reshape(depth, 1, C_WIDTH)
    k_a = c_k_a.reshape(depth, 1, C_WIDTH)
    r_k = c_r_k.reshape(depth, 1, C_WIDTH)
    lnw = c_lnx_w.reshape(depth, 1, C_WIDTH)
    lnb = c_lnx_b.reshape(depth, 1, C_WIDTH)
    woa, wob, woc, wout = (w.astype(BF16) for w in (w_oa, w_ob, w_oc, w_out))
    hid = np.arange(2 * LANES) // HEAD_DIM
    ones_bd = jnp.asarray((hid[:, None] == hid[None, :]).astype(np.float32)).astype(BF16)
    tab_a, tab_b = _rope_tables(tl)

    rows = -(-(bl + 1) // SUBLANES) * SUBLANES
    cond = jnp.zeros((rows, d), F32).at[0:bl].set(c).at[bl].set(c_ctx)
    mod = _modulation(cond, w_mod, b_mod)

    def mod_parts(l, lo, hi):
        m = mod[l, lo:hi]
        return tuple(m[:, j * d:(j + 1) * d].reshape(hi - lo, 1, d) for j in range(3))

    zeros_state = jnp.zeros((bc, N_PAIRS, LANES, LANES), F32)
    cache_kr_pad = jnp.pad(cache_b_krope, ((0, 0), (0, 0), (0, 0), (B_NOPE, LANES - B_NOPE - B_ROPE)))
    cache_kv = jnp.concatenate([cache_a_k.reshape(bl, depth, past, LANES),
                                cache_a_v.reshape(bl, depth, past, LANES),
                                cache_b_ckv, cache_kr_pad], axis=-1)

    xp = x_prompt.reshape(1, bc * tc, d)
    xs = x_sample
    new_ak, new_av, new_ckv, new_kr, new_sf, new_sb = [], [], [], [], [], []

    def mixer_tail(x, gate, ub, uf, ka, va, kb, vb, tabs_q, h0f, h0b, l, nb_tok):
        b_, t_ = ub.shape[0], ub.shape[1]
        ya = _attn(ub, ka, va, tabs_q[0], qnw[l], mode="A", q_off=OFF_AQ, z_off=OFF_AZ)
        yb = _attn(ub, kb, vb, tabs_q[1], None, mode="B", q_off=OFF_BQ, z_off=OFF_BZ)
        prep = _rwkv_prep(uf, mup[l], mun[l], w0[l], a0[l], wup[l], aup[l], k_k[l], k_a[l], r_k[l], ones_bd)
        yf, ybw, hff, hfb = _rwkv_scan(prep, h0f, h0b)
        rs = lambda z: z.reshape(nb_tok, (b_ * t_) // nb_tok, z.shape[-1])
        x_new = _merge(x, gate, rs(ya), rs(yb), rs(yf), rs(ybw), rs(prep[3]), rs(ub),
                       woa[l], wob[l], woc[l], wout[l], lnw[l], lnb[l], ones_bd)
        return x_new, hff, hfb

    for l in range(depth):
        nw = norm_w[l].reshape(1, d)
        shift, scale, gate = mod_parts(l, bl, bl + 1)
        uf, ub = _proj(xp, shift, scale, nw, w_all[l])
        uf, ub = uf.reshape(bc, tc, NF), ub.reshape(bc, tc, NB16)
        ka, va, kb, vb, k_n, ckv_n = _kv_prep(uf, CIN_PAD // KV_W, None, None, knw[l], cnw[l], wuk[l], wuv[l],
                                               emit=True)
        xp, hff, hfb = mixer_tail(xp, gate, ub, uf, ka, va, kb, vb, (None, None), zeros_state, zeros_state, l, 1)
        new_ak.append(k_n.reshape(bc, tc, A_KV_HEADS, HEAD_DIM))
        new_av.append(uf[:, :, CIN_PAD + LANES:CIN_PAD + 2 * LANES].reshape(bc, tc, A_KV_HEADS, HEAD_DIM))
        new_ckv.append(ckv_n)
        new_kr.append(uf[:, :, CIN_PAD + 3 * LANES + B_NOPE:CIN_PAD + 3 * LANES + B_NOPE + B_ROPE])
        new_sf.append(_pairs_to_state(hff))
        new_sb.append(_pairs_to_state(hfb))
        shift, scale, gate = mod_parts(l, 0, bl)
        uf, ub = _proj(xs, shift, scale, nw, w_all[l])
        ka, va, kb, vb = _kv_prep(uf, CIN_PAD // KV_W, cache_kv[:, l], tab_a + tab_b,
                                  knw[l], cnw[l], wuk[l], wuv[l], emit=False)
        xs, _, _ = mixer_tail(xs, gate, ub, uf, ka, va, kb, vb, (tab_a, tab_b),
                              _state_to_pairs(state_c_fwd[:, l]), _state_to_pairs(state_c_bwd[:, l]), l, bl)

    y_prompt = _final_norm(xp, final_norm_w).reshape(bc, tc, d)
    y_sample = _final_norm(xs, final_norm_w)
    return (y_prompt, y_sample,
            jnp.stack(new_ak, axis=1), jnp.stack(new_av, axis=1),
            jnp.stack(new_ckv, axis=1), jnp.stack(new_kr, axis=1),
            jnp.stack(new_sf, axis=1), jnp.stack(new_sb, axis=1))
```

```python
import functools
import math

import numpy as np
import jax
import jax.numpy as jnp
from jax import lax
from jax.experimental import pallas as pl
from jax.experimental.pallas import tpu as pltpu

F32 = jnp.float32
BF16 = jnp.bfloat16

D_MODEL = 1024
GRID_W = 64
HEAD_DIM = 64
N_HEADS = 8
A_KV_HEADS = 2
A_GROUPS = N_HEADS // A_KV_HEADS
B_NOPE = 64
B_ROPE = 32
B_KV_RANK = 128
C_WIDTH = N_HEADS * HEAD_DIM
C_LORA = 64
C_SHIFT_DIM = 3 * C_WIDTH + 4 * C_LORA
ROPE_THETA = 10000.0
NORM_EPS = 1e-6
C_GN_EPS = 64e-5
LOG2_E = math.log2(math.e)

LANES = 128
SUBLANES = 8
VMEM_LIMIT_BYTES = 56 * 1024 * 1024

PROJ_COL_CHUNK = 512
ATTN_Q_TILE = 512
ATTN_PAIRS_PER_STEP = 2
ATTN_SUB_TILE = 256
ATTN_KEY_CHUNK = 256
ATTN_SKEW = 2
VT_ROWS = HEAD_DIM + 16
CHUNK = 64
N_PAIRS = N_HEADS // 2
WIDE = N_HEADS * LANES

CIN_PAD = 2048
KV_W = 4 * LANES
NF = CIN_PAD + KV_W
G_W = 3 * D_MODEL
OFF_G = 0
OFF_AQ = OFF_G + G_W
OFF_AZ = OFF_AQ + WIDE
OFF_BQ = OFF_AZ + C_WIDTH
OFF_BZ = OFF_BQ + WIDE
OFF_CZ = OFF_BZ + C_WIDTH
NB16 = OFF_CZ + C_WIDTH

_R_AQ, _R_AK, _R_AV, _R_AZ = 0, 512, 640, 768
_R_BQ, _R_CKV, _R_KR, _R_BZ = 1280, 2048, 2176, 2208
_R_CIN, _R_CZ, _R_G = 2720, 4512, 5024


def _column_maps():
    f = np.full((NF,), -1, np.int64)
    f[0:C_SHIFT_DIM] = _R_CIN + np.arange(C_SHIFT_DIM)
    f[CIN_PAD:CIN_PAD + 128] = _R_AK + np.arange(128)
    f[CIN_PAD + 128:CIN_PAD + 256] = _R_AV + np.arange(128)
    f[CIN_PAD + 256:CIN_PAD + 384] = _R_CKV + np.arange(128)
    f[CIN_PAD + 384 + 64:CIN_PAD + 384 + 96] = _R_KR + np.arange(32)
    b = np.full((NB16,), -1, np.int64)
    b[OFF_G:OFF_G + G_W] = _R_G + np.arange(G_W)
    for h in range(N_HEADS):
        kv = h // A_GROUPS
        lo = OFF_AQ + h * LANES + kv * HEAD_DIM
        b[lo:lo + HEAD_DIM] = _R_AQ + h * HEAD_DIM + np.arange(HEAD_DIM)
        lo = OFF_BQ + h * LANES
        b[lo:lo + B_NOPE + B_ROPE] = _R_BQ + h * (B_NOPE + B_ROPE) + np.arange(B_NOPE + B_ROPE)
    b[OFF_AZ:OFF_AZ + C_WIDTH] = _R_AZ + np.arange(C_WIDTH)
    b[OFF_BZ:OFF_BZ + C_WIDTH] = _R_BZ + np.arange(C_WIDTH)
    b[OFF_CZ:OFF_CZ + C_WIDTH] = _R_CZ + np.arange(C_WIDTH)
    return f, b


def _gather_cols(w, idx):
    pieces, start = [], 0
    for pos in range(1, len(idx) + 1):
        run_ends = pos == len(idx) or (idx[pos] != idx[pos - 1] + 1 if idx[pos - 1] >= 0 else idx[pos] >= 0)
        if run_ends:
            if idx[start] >= 0:
                pieces.append(w[..., int(idx[start]):int(idx[pos - 1]) + 1])
            else:
                pieces.append(jnp.zeros(w.shape[:-1] + (pos - start,), w.dtype))
            start = pos
    return jnp.concatenate(pieces, axis=-1)


def _tile(n, pref):
    t = min(n, pref)
    assert n % t == 0, (n, pref)
    return t


def _cparams(sem):
    return pltpu.CompilerParams(dimension_semantics=sem, vmem_limit_bytes=VMEM_LIMIT_BYTES)


def _silu(z):
    return z * jax.nn.sigmoid(z)


def _dot(a, b):
    return jnp.dot(a, b, preferred_element_type=F32)


def _dot_nt(a, b):
    return lax.dot_general(a, b, (((1,), (1,)), ((), ())), preferred_element_type=F32)


def _dot_tn(a, b):
    return lax.dot_general(a, b, (((0,), (0,)), ((), ())), preferred_element_type=F32)


def _head_sum(x, ones_bd):
    hi = x.astype(BF16)
    lo = (x - hi.astype(F32)).astype(BF16)
    w = ones_bd.shape[0]
    halves = [_dot(hi[:, c:c + w], ones_bd) + _dot(lo[:, c:c + w], ones_bd) for c in range(0, x.shape[1], w)]
    return jnp.concatenate(halves, axis=1)


def _rope(x, cos, s1, s2, shift):
    n = x.shape[-1]
    return x * cos + pltpu.roll(x, n - shift, 1) * s1 + pltpu.roll(x, shift, 1) * s2


def _mod_kernel(cond_ref, w_ref, b_ref, o_ref):
    s = _silu(cond_ref[...]).astype(BF16)
    o_ref[0] = _dot(s, w_ref[0].astype(BF16)) + b_ref[0]


def _modulation(cond, w_mod, b_mod):
    depth, d, n = w_mod.shape
    rows = cond.shape[0]
    tn = _tile(n, 1024)
    return pl.pallas_call(
        _mod_kernel,
        grid=(depth, n // tn),
        in_specs=[pl.BlockSpec((rows, d), lambda l, j: (0, 0)),
                  pl.BlockSpec((1, d, tn), lambda l, j: (l, 0, j)),
                  pl.BlockSpec((1, 1, tn), lambda l, j: (l, 0, j))],
        out_specs=pl.BlockSpec((1, rows, tn), lambda l, j: (l, 0, j)),
        out_shape=jax.ShapeDtypeStruct((depth, rows, n), F32),
        compiler_params=_cparams(("parallel", "parallel")),
        name="modulation",
    )(cond, w_mod, b_mod.reshape(depth, 1, n))


def _proj_kernel(x_ref, sh_ref, sc_ref, nw_ref, w_ref, of_ref, ob_ref):
    x = x_ref[0]
    ms = jnp.mean(x * x, axis=-1, keepdims=True)
    y = x * lax.rsqrt(ms + NORM_EPS) * nw_ref[...]
    h = (y * (1.0 + sc_ref[0]) + sh_ref[0]).astype(BF16)
    for c0 in range(0, NF, PROJ_COL_CHUNK):
        of_ref[0, :, c0:c0 + PROJ_COL_CHUNK] = _dot(h, w_ref[:, c0:c0 + PROJ_COL_CHUNK])
    for c0 in range(0, NB16, PROJ_COL_CHUNK):
        ob_ref[0, :, c0:c0 + PROJ_COL_CHUNK] = _dot(h, w_ref[:, NF + c0:NF + c0 + PROJ_COL_CHUNK]).astype(BF16)


def _proj(x, shift, scale, norm_w, w):
    nb, t, d = x.shape
    tm = _tile(t, 256)
    return pl.pallas_call(
        _proj_kernel,
        grid=(nb, t // tm),
        in_specs=[pl.BlockSpec((1, tm, d), lambda b, i: (b, i, 0)),
                  pl.BlockSpec((1, 1, d), lambda b, i: (b, 0, 0)),
                  pl.BlockSpec((1, 1, d), lambda b, i: (b, 0, 0)),
                  pl.BlockSpec((1, d), lambda b, i: (0, 0)),
                  pl.BlockSpec((d, NF + NB16), lambda b, i: (0, 0), pipeline_mode=pl.Buffered(1))],
        out_specs=[pl.BlockSpec((1, tm, NF), lambda b, i: (b, i, 0)),
                   pl.BlockSpec((1, tm, NB16), lambda b, i: (b, i, 0))],
        out_shape=[jax.ShapeDtypeStruct((nb, t, NF), F32), jax.ShapeDtypeStruct((nb, t, NB16), BF16)],
        compiler_params=_cparams(("parallel", "parallel")),
        name="proj",
    )(x, shift, scale, norm_w, w)


def _kv_prep_tile(kv, tabs, knw, cnw, wuk, wuv, outs, cache_outs, *, norm):
    ka_ref, va_ref, kb_ref, vb_ref = outs
    ak = kv[:, 0:LANES]
    av = kv[:, LANES:2 * LANES]
    ckv = kv[:, 2 * LANES:3 * LANES]
    kr = kv[:, 3 * LANES:4 * LANES]
    lo = lax.broadcasted_iota(jnp.int32, (1, LANES), 1) < HEAD_DIM
    if norm:
        sq = ak * ak
        s0 = jnp.sum(jnp.where(lo, sq, 0.0), axis=-1, keepdims=True)
        s1 = jnp.sum(jnp.where(lo, 0.0, sq), axis=-1, keepdims=True)
        ms = jnp.where(lo, s0, s1) * (1.0 / HEAD_DIM)
        ak = ak * lax.rsqrt(ms + NORM_EPS) * knw
        ckv = ckv * lax.rsqrt(jnp.mean(ckv * ckv, axis=-1, keepdims=True) + NORM_EPS) * cnw
    if cache_outs is not None:
        cache_outs[0][0] = ak
        cache_outs[1][0] = ckv
    if tabs is not None:
        ca, sa1, sa2, cb, sb1, sb2 = tabs
        ak = _rope(ak, ca, sa1, sa2, 16)
        kr = _rope(kr, cb, sb1, sb2, 8)
    ka_ref[0] = ak.astype(BF16)
    tm = kv.shape[0]
    ones_rows = jnp.ones((VT_ROWS - HEAD_DIM, tm), BF16)

    def put_pair(ref, h0, pair):
        pt = jnp.transpose(pair).astype(BF16)
        for e in range(2):
            ref[0, h0 + e, 0:HEAD_DIM, :] = pt[e * HEAD_DIM:(e + 1) * HEAD_DIM]
            ref[0, h0 + e, HEAD_DIM:VT_ROWS, :] = ones_rows

    put_pair(va_ref, 0, av)
    cb16 = ckv.astype(BF16)
    kn = _dot(cb16, wuk)
    for h in range(N_HEADS):
        kb_ref[0, :, h * LANES:(h + 1) * LANES] = (kn[:, h * LANES:(h + 1) * LANES] + kr).astype(BF16)
    vn = _dot(cb16, wuv)
    for p in range(N_PAIRS):
        put_pair(vb_ref, 2 * p, vn[:, p * LANES:(p + 1) * LANES])


def _kv_prep_kernel(*refs, rope, emit, n_new):
    it = iter(refs)
    kv_ref = next(it)
    cache_ref = next(it) if n_new is not None else None
    tabs = tuple(next(it)[...] for _ in range(6)) if rope else None
    knw, cnw, wuk, wuv = (next(it)[...] for _ in range(4))
    outs = (next(it), next(it), next(it), next(it))
    cache_outs = (next(it), next(it)) if emit else None
    if n_new is None:
        _kv_prep_tile(kv_ref[0], tabs, knw, cnw, wuk, wuv, outs, cache_outs, norm=True)
        return
    i = pl.program_id(1)

    @pl.when(i < n_new)
    def _():
        _kv_prep_tile(kv_ref[0], tabs, knw, cnw, wuk, wuv, outs, cache_outs, norm=True)

    @pl.when(i >= n_new)
    def _():
        _kv_prep_tile(cache_ref[0], None, knw, cnw, wuk, wuv, outs, None, norm=False)


def _kv_prep(kv, col_block, cache, tabs, knw, cnw, wuk, wuv, *, emit):
    b, t = kv.shape[0], kv.shape[1]
    rope = tabs is not None
    if cache is None:
        tm, n_new, total = _tile(t, 512), None, t
        in_specs = [pl.BlockSpec((1, tm, KV_W), lambda bb, i: (bb, i, col_block))]
        args = [kv]
        tab_spec = pl.BlockSpec((tm, LANES), lambda bb, i: (i, 0))
    else:
        past = cache.shape[1]
        tm = _tile(past, 512)
        assert t % tm == 0
        n_new, total = t // tm, t + past
        in_specs = [pl.BlockSpec((1, tm, KV_W), lambda bb, i: (bb, jnp.minimum(i, n_new - 1), col_block)),
                    pl.BlockSpec((1, tm, KV_W), lambda bb, i: (bb, jnp.maximum(i - n_new, 0), 0))]
        args = [kv, cache]
        tab_spec = pl.BlockSpec((tm, LANES), lambda bb, i: (jnp.minimum(i, n_new - 1), 0))
    if rope:
        in_specs += [tab_spec] * 6
        args += list(tabs)
    const = lambda bb, i: (0, 0)
    in_specs += [pl.BlockSpec((1, LANES), const), pl.BlockSpec((1, LANES), const),
                 pl.BlockSpec((B_KV_RANK, WIDE), const), pl.BlockSpec((B_KV_RANK, C_WIDTH), const)]
    args += [knw, cnw, wuk, wuv]

    def rows_out(w, dt):
        return pl.BlockSpec((1, tm, w), lambda bb, i: (bb, i, 0)), jax.ShapeDtypeStruct((b, total, w), dt)

    def vt_out(nh):
        return (pl.BlockSpec((1, nh, VT_ROWS, tm), lambda bb, i: (bb, 0, 0, i)),
                jax.ShapeDtypeStruct((b, nh, VT_ROWS, total), BF16))

    outs = [rows_out(LANES, BF16), vt_out(A_KV_HEADS), rows_out(WIDE, BF16), vt_out(N_HEADS)]
    if emit:
        outs += [rows_out(LANES, F32), rows_out(LANES, F32)]
    out_specs = [o[0] for o in outs]
    out_shape = [o[1] for o in outs]
    return pl.pallas_call(
        functools.partial(_kv_prep_kernel, rope=rope, emit=emit, n_new=n_new),
        grid=(b, total // tm),
        in_specs=in_specs, out_specs=out_specs, out_shape=out_shape,
        compiler_params=_cparams(("parallel", "parallel")),
        name="kv_prep",
    )(*args)


def _attn_kernel(*refs, mode, rope, scale):
    it = iter(refs)
    q_ref, z_ref, k_ref, vt_ref = next(it), next(it), next(it), next(it)
    if rope:
        cos, s1, s2 = (next(it)[...] for _ in range(3))
    if mode == "A":
        qnw = next(it)[...]
    o_ref = next(it)

    tq = q_ref.shape[1]
    tw = min(tq, ATTN_SUB_TILE)
    s_len = k_ref.shape[1]
    ck = ATTN_KEY_CHUNK if s_len % ATTN_KEY_CHUNK == 0 else LANES
    n_chunks = s_len // ck
    n_pairs = q_ref.shape[2] // (2 * LANES)
    streams = [(e, r) for r in range(tq // tw) for e in range(2)]
    qs = {}

    def prep_queries(pr):
        for e, r in streams:
            rs = slice(r * tw, (r + 1) * tw)
            c0 = (2 * pr + e) * LANES
            qh = q_ref[0, rs, c0:c0 + LANES].astype(F32)
            if mode == "A":
                ms = jnp.sum(qh * qh, axis=-1, keepdims=True) * (1.0 / HEAD_DIM)
                qh = qh * lax.rsqrt(ms + NORM_EPS) * qnw
            if rope:
                qh = _rope(qh, cos[rs], s1[rs], s2[rs], 16 if mode == "A" else 8)
            qs[pr, e, r] = (qh * (scale * LOG2_E)).astype(BF16)

    m, acc, st = {}, {}, {}

    def scores(c, t):
        pr, j = divmod(t, n_chunks)
        e = c[0]
        h = 2 * pr + e
        kh = k_ref[0, j * ck:(j + 1) * ck, :] if mode == "A" else k_ref[0, j * ck:(j + 1) * ck, h * LANES:(h + 1) * LANES]
        st[c, t] = _dot_nt(kh, qs[(pr,) + c])

    def consume(c, t):
        pr, j = divmod(t, n_chunks)
        e, r = c
        if j == 0:
            m[c] = jnp.full((1, tw), -1e30, F32)
            acc[c] = jnp.zeros((VT_ROWS, tw), F32)
        s = st.pop((c, t))
        m_new = jnp.maximum(m[c], jnp.max(s, axis=0, keepdims=True))
        p = jnp.exp2(s - m_new).astype(BF16)
        vt = vt_ref[0, 0 if mode == "A" else 2 * pr + e, :, j * ck:(j + 1) * ck]
        acc[c] = acc[c] * jnp.exp2(m[c] - m_new) + _dot(vt, p)
        m[c] = m_new
        if j == n_chunks - 1 and e == 1:
            rs = slice(r * tw, (r + 1) * tw)
            o_t = jnp.concatenate([acc[ee, r][0:HEAD_DIM] / acc[ee, r][HEAD_DIM:HEAD_DIM + 1] for ee in range(2)],
                                  axis=0)
            gate = _silu(z_ref[0, rs, pr * LANES:(pr + 1) * LANES].astype(F32))
            o_ref[0, rs, pr * LANES:(pr + 1) * LANES] = (jnp.transpose(o_t) * gate).astype(o_ref.dtype)

    total = n_pairs * n_chunks
    prep_queries(0)
    for t in range(total + ATTN_SKEW):
        if t < total:
            pr, j = divmod(t, n_chunks)
            if j == n_chunks // 2 and pr + 1 < n_pairs:
                prep_queries(pr + 1)
            for c in streams:
                scores(c, t)
        if t >= ATTN_SKEW:
            for c in streams:
                consume(c, t - ATTN_SKEW)


def _attn(ub, k, v, tabs, qnw, *, mode, q_off, z_off):
    b, tq_all = ub.shape[0], ub.shape[1]
    s_len = k.shape[1]
    tq = _tile(tq_all, ATTN_Q_TILE)
    rope = tabs is not None
    npp = ATTN_PAIRS_PER_STEP
    assert N_PAIRS % npp == 0 and A_GROUPS % (2 * npp) == 0
    qw, zw = 2 * npp * LANES, npp * LANES
    qb0, zb0 = q_off // qw, z_off // zw
    in_specs = [pl.BlockSpec((1, tq, qw), lambda bb, p, i: (bb, i, qb0 + p)),
                pl.BlockSpec((1, tq, zw), lambda bb, p, i: (bb, i, zb0 + p))]
    if mode == "A":
        kv_of = A_GROUPS // (2 * npp)
        in_specs += [pl.BlockSpec((1, s_len, LANES), lambda bb, p, i: (bb, 0, 0)),
                     pl.BlockSpec((1, 1, VT_ROWS, s_len), lambda bb, p, i: (bb, p // kv_of, 0, 0))]
        scale = HEAD_DIM ** -0.5
    else:
        in_specs += [pl.BlockSpec((1, s_len, qw), lambda bb, p, i: (bb, 0, p)),
                     pl.BlockSpec((1, 2 * npp, VT_ROWS, s_len), lambda bb, p, i: (bb, p, 0, 0))]
        scale = (B_NOPE + B_ROPE) ** -0.5
    args = [ub, ub, k, v]
    if rope:
        in_specs += [pl.BlockSpec((tq, LANES), lambda bb, p, i: (i, 0))] * 3
        args += list(tabs)
    if mode == "A":
        in_specs += [pl.BlockSpec((1, LANES), lambda bb, p, i: (0, 0))]
        args += [qnw]
    return pl.pallas_call(
        functools.partial(_attn_kernel, mode=mode, rope=rope, scale=scale),
        grid=(b, N_PAIRS // npp, tq_all // tq),
        in_specs=in_specs,
        out_specs=pl.BlockSpec((1, tq, zw), lambda bb, p, i: (bb, i, p)),
        out_shape=jax.ShapeDtypeStruct((b, tq_all, C_WIDTH), BF16),
        compiler_params=_cparams(("parallel", "parallel", "parallel")),
        name="attn_" + mode,
    )(*args)


def _rwkv_prep_kernel(cin_ref, hp_ref, hn_ref, mup_ref, mun_ref, w0_ref, a0_ref, wup_ref, aup_ref,
                      kk_ref, ka_ref, rk_ref, ones_ref,
                      r_o, v_o, kk_o, bonus_o, lwf_o, kf_o, bf_o, lwb_o, kb_o, bb_o):
    i = pl.program_id(1)
    n = pl.num_programs(1)
    s = cin_ref[0]
    tm = s.shape[0]
    prev_row = jnp.where(i > 0, hp_ref[0, SUBLANES - 1:SUBLANES, :], 0.0)
    next_row = jnp.where(i < n - 1, hn_ref[0, 0:1, :], 0.0)
    rows = lax.broadcasted_iota(jnp.int32, (tm, 1), 0)
    prev = jnp.where(rows == 0, prev_row, pltpu.roll(s, 1, 0))
    nxt = jnp.where(rows == tm - 1, next_row, pltpu.roll(s, tm - 1, 0))
    x = s + mup_ref[...] * (prev - s) + mun_ref[...] * (nxt - s)

    w = C_WIDTH
    r = x[:, 0:w]
    k = x[:, w:2 * w]
    v = x[:, 2 * w:3 * w]
    wd = jnp.tanh(x[:, 3 * w:3 * w + LANES]).astype(BF16)
    ad = x[:, 3 * w + LANES:3 * w + 2 * LANES].astype(BF16)
    w_raw = w0_ref[...] + _dot(wd, wup_ref[...])
    logw = (-math.exp(-0.5)) * jax.nn.sigmoid(w_raw)
    a = jax.nn.sigmoid(a0_ref[...] + _dot(ad, aup_ref[...]))
    ones_bd = ones_ref[...]
    kk = k * kk_ref[...]
    kk = kk / jnp.maximum(jnp.sqrt(_head_sum(kk * kk, ones_bd)), 1e-12)
    ka = ka_ref[...]
    a_f, a_b = a[:, 0:w], a[:, w:2 * w]
    k_f = k * (1.0 + (a_f - 1.0) * ka)
    k_b = k * (1.0 + (a_b - 1.0) * ka)
    r_o[0] = r
    v_o[0] = v
    kk_o[0] = kk
    bonus_o[0] = _head_sum(r * (k_f + k_b) * rk_ref[...], ones_bd) * v
    lwf_o[0] = logw[:, 0:w]
    kf_o[0] = k_f
    bf_o[0] = kk * a_f
    lwb_o[0] = logw[:, w:2 * w]
    kb_o[0] = k_b
    bb_o[0] = kk * a_b


def _rwkv_prep(uf, mup, mun, w0, a0, wup, aup, k_k, k_a, r_k, ones_bd):
    b, t = uf.shape[0], uf.shape[1]
    tm = _tile(t, 256)
    nblk8 = t // SUBLANES
    step8 = tm // SUBLANES
    const = lambda bb, i: (0, 0)
    in_specs = [
        pl.BlockSpec((1, tm, CIN_PAD), lambda bb, i: (bb, i, 0)),
        pl.BlockSpec((1, SUBLANES, CIN_PAD), lambda bb, i: (bb, jnp.maximum(i * step8 - 1, 0), 0)),
        pl.BlockSpec((1, SUBLANES, CIN_PAD), lambda bb, i: (bb, jnp.minimum((i + 1) * step8, nblk8 - 1), 0)),
        pl.BlockSpec((1, CIN_PAD), const), pl.BlockSpec((1, CIN_PAD), const),
        pl.BlockSpec((1, 2 * C_WIDTH), const), pl.BlockSpec((1, 2 * C_WIDTH), const),
        pl.BlockSpec((LANES, 2 * C_WIDTH), const), pl.BlockSpec((LANES, 2 * C_WIDTH), const),
        pl.BlockSpec((1, C_WIDTH), const), pl.BlockSpec((1, C_WIDTH), const), pl.BlockSpec((1, C_WIDTH), const),
        pl.BlockSpec((2 * LANES, 2 * LANES), const),
    ]
    out_specs = [pl.BlockSpec((1, tm, C_WIDTH), lambda bb, i: (bb, i, 0))] * 10
    out_shape = [jax.ShapeDtypeStruct((b, t, C_WIDTH), F32)] * 10
    return pl.pallas_call(
        _rwkv_prep_kernel,
        grid=(b, t // tm),
        in_specs=in_specs, out_specs=out_specs, out_shape=out_shape,
        compiler_params=_cparams(("parallel", "parallel")),
        name="rwkv_prep",
    )(uf, uf, uf, mup, mun, w0, a0, wup, aup, k_k, k_a, r_k, ones_bd)


def _chunk_operands(r, v, kk, logw, kd, bd, rev):
    c = CHUNK
    rows = lax.broadcasted_iota(jnp.int32, (c, 1), 0)
    cum = logw
    shift = 1
    while shift < c:
        if rev:
            cum = cum + jnp.where(rows < c - shift, pltpu.roll(cum, c - shift, 0), 0.0)
        else:
            cum = cum + jnp.where(rows >= shift, pltpu.roll(cum, shift, 0), 0.0)
        shift *= 2
    tot = cum[0:1, :] if rev else cum[c - 1:c, :]
    e_neg = jnp.exp(-cum)
    e_rem = jnp.exp(tot - cum)
    return dict(a=-kk * jnp.exp(cum - logw), r=r * jnp.exp(cum), b=bd * e_neg, k=kd * e_neg,
                bc=bd * e_rem, kc=kd * e_rem, v=v, g_tot=jnp.exp(tot))


def _scan_chunk(dirs, h_scr, y_refs):
    c = CHUNK
    lane = lax.broadcasted_iota(jnp.int32, (1, LANES), 1)
    m_lo = jnp.where(lane < HEAD_DIM, 1.0, 0.0)
    m_hi = 1.0 - m_lo
    row = lax.broadcasted_iota(jnp.int32, (LANES, LANES), 0)
    col = lax.broadcasted_iota(jnp.int32, (LANES, LANES), 1)
    same = (row // c) == (col // c)
    diag = row == col
    eye = jnp.where(diag, 1.0, 0.0)

    chains = []
    for d_idx, (ops, rev) in enumerate(dirs):
        before = (col > row) if rev else (col < row)
        strict = same & before
        incl = same & (before | diag)
        for p in range(N_PAIRS):
            sl = slice(p * LANES, (p + 1) * LANES)

            def stack(z, sl=sl):
                zp = z[:, sl]
                return jnp.concatenate([zp * m_lo, zp * m_hi], axis=0)

            ch = {name: stack(ops[name]) for name in ("a", "r", "b", "k", "bc", "kc", "v")}
            ch.update(d=d_idx, sl=sl, strict=strict, incl=incl, g_tot=ops["g_tot"][:, sl])
            chains.append(ch)

    for ch in chains:
        ar = jnp.concatenate([ch["a"], ch["r"]], axis=0).astype(BF16)
        bk = jnp.concatenate([ch["b"], ch["k"]], axis=0).astype(BF16)
        ch["sc"] = _dot_nt(ar, bk)
    for ch in chains:
        sc = ch.pop("sc")
        ch["a_ab"] = jnp.where(ch["strict"], sc[0:LANES, 0:LANES], 0.0)
        a_ak = jnp.where(ch["strict"], sc[0:LANES, LANES:2 * LANES], 0.0)
        ch["m_rb"] = jnp.where(ch["incl"], sc[LANES:2 * LANES, 0:LANES], 0.0).astype(BF16)
        m_rk = jnp.where(ch["incl"], sc[LANES:2 * LANES, LANES:2 * LANES], 0.0)
        ch["vb"] = ch["v"].astype(BF16)
        xm = _dot(jnp.concatenate([a_ak, m_rk], axis=0).astype(BF16), ch["vb"])
        ch["x"] = xm[0:LANES]
        ch["mv"] = xm[LANES:2 * LANES]
        ch["kv"] = _dot_tn(ch["kc"].astype(BF16), ch["vb"])
        ch["t"] = eye + ch["a_ab"]
    for ch in chains:
        ab = ch.pop("a_ab").astype(BF16)
        ch["pw"] = _dot(ab, ab)
    n_levels = int(math.log2(c)) - 1
    for lvl in range(n_levels):
        last = lvl == n_levels - 1
        for ch in chains:
            pwb = ch["pw"].astype(BF16)
            tb = ch["t"].astype(BF16)
            if last:
                ch["t"] = ch["t"] + _dot(pwb, tb)
            else:
                nxt = _dot(pwb, jnp.concatenate([pwb, tb], axis=1))
                ch["pw"] = nxt[:, 0:LANES]
                ch["t"] = ch["t"] + nxt[:, LANES:2 * LANES]
    for ch in chains:
        ax = jnp.concatenate([ch["a"], ch["x"]], axis=1).astype(BF16)
        ch["g"] = _dot(ch["t"].astype(BF16), ax).astype(BF16)
    for ch in chains:
        ch["ry"] = jnp.concatenate([ch["r"], ch["mv"]], axis=1) + _dot(ch["m_rb"], ch["g"])
        ch["pq"] = _dot_tn(ch["bc"].astype(BF16), ch["g"])
    for ch in chains:
        q_mat = ch["pq"][:, LANES:2 * LANES] + ch["kv"]
        d_idx, p = ch["d"], ch["sl"].start // LANES
        h_old = h_scr[d_idx, p]
        rp = jnp.concatenate([ch["ry"][:, 0:LANES], ch["pq"][:, 0:LANES]], axis=0).astype(BF16)
        yh = _dot(rp, h_old.astype(BF16))
        y2 = ch["ry"][:, LANES:2 * LANES] + yh[0:LANES]
        g_rows = jnp.transpose(jnp.broadcast_to(ch["g_tot"], (LANES, LANES)))
        h_scr[d_idx, p] = g_rows * h_old + yh[LANES:2 * LANES] + q_mat
        y_refs[d_idx][0, :, ch["sl"]] = y2[0:c, :] + y2[c:2 * c, :]


def _rwkv_scan_kernel(rf, vf, kkf, lwf, kf, bf, rb, vb, kkb, lwb, kb, bb, h0f, h0b,
                      yf, yb, hff, hfb, h_scr):
    i = pl.program_id(1)

    @pl.when(i == 0)
    def _():
        h_scr[0] = h0f[0]
        h_scr[1] = h0b[0]

    fwd = _chunk_operands(rf[0], vf[0], kkf[0], lwf[0], kf[0], bf[0], rev=False)
    bwd = _chunk_operands(rb[0], vb[0], kkb[0], lwb[0], kb[0], bb[0], rev=True)
    _scan_chunk([(fwd, False), (bwd, True)], h_scr, (yf, yb))

    @pl.when(i == pl.num_programs(1) - 1)
    def _():
        hff[0] = h_scr[0]
        hfb[0] = h_scr[1]


def _rwkv_scan(prep, h0f, h0b):
    r, v, kk, _, lwf, kf, bf, lwb, kb, bb = prep
    b, t = r.shape[0], r.shape[1]
    nc = t // CHUNK
    fwd = pl.BlockSpec((1, CHUNK, C_WIDTH), lambda bb_, i: (bb_, i, 0))
    bwd = pl.BlockSpec((1, CHUNK, C_WIDTH), lambda bb_, i: (bb_, nc - 1 - i, 0))
    st = pl.BlockSpec((1, N_PAIRS, LANES, LANES), lambda bb_, i: (bb_, 0, 0, 0))
    return pl.pallas_call(
        _rwkv_scan_kernel,
        grid=(b, nc),
        in_specs=[fwd] * 6 + [bwd] * 6 + [st, st],
        out_specs=[fwd, bwd, st, st],
        out_shape=[jax.ShapeDtypeStruct((b, t, C_WIDTH), F32)] * 2
        + [jax.ShapeDtypeStruct((b, N_PAIRS, LANES, LANES), F32)] * 2,
        scratch_shapes=[pltpu.VMEM((2, N_PAIRS, LANES, LANES), F32)],
        compiler_params=_cparams(("parallel", "arbitrary")),
        name="rwkv_scan",
    )(r, v, kk, lwf, kf, bf, r, v, kk, lwb, kb, bb, h0f, h0b)


def _state_to_pairs(s):
    h = jnp.swapaxes(s.astype(F32), -1, -2).reshape(s.shape[0], N_PAIRS, 2, HEAD_DIM, HEAD_DIM)
    z = jnp.zeros_like(h[:, :, 0])
    top = jnp.concatenate([h[:, :, 0], z], axis=-1)
    bot = jnp.concatenate([z, h[:, :, 1]], axis=-1)
    return jnp.concatenate([top, bot], axis=-2)


def _pairs_to_state(hp):
    e = hp[:, :, 0:HEAD_DIM, 0:HEAD_DIM]
    o = hp[:, :, HEAD_DIM:, HEAD_DIM:]
    h = jnp.stack([e, o], axis=2).reshape(hp.shape[0], N_HEADS, HEAD_DIM, HEAD_DIM)
    return jnp.swapaxes(h, -1, -2)


def _merge_kernel(x_ref, gate_ref, ya_ref, yb_ref, yf_ref, ybw_ref, bonus_ref, cz_ref, g_ref,
                  woa_ref, wob_ref, woc_ref, wout_ref, lnw_ref, lnb_ref, ones_ref, o_ref):
    ones_bd = ones_ref[...]
    y = yf_ref[0] + ybw_ref[0]
    mu = _head_sum(y, ones_bd) * (1.0 / HEAD_DIM)
    yc = y - mu
    var = _head_sum(yc * yc, ones_bd) * (1.0 / HEAD_DIM)
    yn = yc * lax.rsqrt(var + C_GN_EPS) * lnw_ref[...] + lnb_ref[...] + bonus_ref[0]
    ycg = (yn * _silu(cz_ref[0].astype(F32))).astype(BF16)
    d = D_MODEL
    ga = jax.nn.sigmoid(g_ref[0, :, 0:d].astype(F32))
    gb = jax.nn.sigmoid(g_ref[0, :, d:2 * d].astype(F32))
    gc = jax.nn.sigmoid(g_ref[0, :, 2 * d:3 * d].astype(F32))
    mixed = (ga * _dot(ya_ref[0], woa_ref[...]) + gb * _dot(yb_ref[0], wob_ref[...])
             + gc * _dot(ycg, woc_ref[...]))
    out = _dot(mixed.astype(BF16), wout_ref[...])
    o_ref[0] = x_ref[0] + gate_ref[0] * out


def _merge(x, gate, ya, yb, yf, ybw, bonus, ub, woa, wob, woc, wout, lnw, lnb, ones_bd):
    nb, t, d = x.shape
    tm = _tile(t, 256)
    tok = lambda w: pl.BlockSpec((1, tm, w), lambda b, i: (b, i, 0))
    const = lambda b, i: (0, 0)
    in_specs = [tok(d), pl.BlockSpec((1, 1, d), lambda b, i: (b, 0, 0)),
                tok(C_WIDTH), tok(C_WIDTH), tok(C_WIDTH), tok(C_WIDTH), tok(C_WIDTH),
                pl.BlockSpec((1, tm, C_WIDTH), lambda b, i: (b, i, OFF_CZ // C_WIDTH)),
                pl.BlockSpec((1, tm, G_W), lambda b, i: (b, i, OFF_G // G_W)),
                pl.BlockSpec((C_WIDTH, d), const), pl.BlockSpec((C_WIDTH, d), const),
                pl.BlockSpec((C_WIDTH, d), const), pl.BlockSpec((d, d), const),
                pl.BlockSpec((1, C_WIDTH), const), pl.BlockSpec((1, C_WIDTH), const),
                pl.BlockSpec((2 * LANES, 2 * LANES), const)]
    return pl.pallas_call(
        _merge_kernel,
        grid=(nb, t // tm),
        in_specs=in_specs,
        out_specs=tok(d),
        out_shape=jax.ShapeDtypeStruct((nb, t, d), F32),
        compiler_params=_cparams(("parallel", "parallel")),
        name="merge",
    )(x, gate, ya, yb, yf, ybw, bonus, ub, ub, woa, wob, woc, wout, lnw, lnb, ones_bd)


def _final_norm_kernel(x_ref, w_ref, o_ref):
    x = x_ref[...]
    o_ref[...] = x * lax.rsqrt(jnp.mean(x * x, axis=-1, keepdims=True) + NORM_EPS) * w_ref[...]


def _final_norm(x, w):
    shape = x.shape
    x2 = x.reshape(-1, shape[-1])
    n, d = x2.shape
    tm = _tile(n, 1024)
    out = pl.pallas_call(
        _final_norm_kernel,
        grid=(n // tm,),
        in_specs=[pl.BlockSpec((tm, d), lambda i: (i, 0)), pl.BlockSpec((1, d), lambda i: (0, 0))],
        out_specs=pl.BlockSpec((tm, d), lambda i: (i, 0)),
        out_shape=jax.ShapeDtypeStruct((n, d), F32),
        compiler_params=_cparams(("parallel",)),
        name="final_norm",
    )(x2, w.reshape(1, d))
    return out.reshape(shape)


def _rope_tables(n_tokens):
    t = np.arange(n_tokens)
    pos = np.stack([t // GRID_W, t % GRID_W], axis=0).astype(np.float32)
    lane = np.arange(LANES)

    def build(active, part, freq_idx, half, first):
        inv = jnp.asarray(ROPE_THETA, F32) ** (-jnp.asarray(freq_idx, F32) / half)
        posm = jnp.asarray(pos)[jnp.asarray(part)]
        ang = posm.T * inv[None, :]
        cos = jnp.where(jnp.asarray(active)[None, :], jnp.cos(ang), 1.0)
        sin = jnp.where(jnp.asarray(active)[None, :], jnp.sin(ang), 0.0)
        s1 = jnp.where(jnp.asarray(first)[None, :], -sin, 0.0)
        s2 = jnp.where(jnp.asarray(first)[None, :], 0.0, sin)
        return cos.astype(F32), s1.astype(F32), s2.astype(F32)

    i = lane % HEAD_DIM
    tab_a = build(np.ones(LANES, bool), i // 32, (i % 32) % 16, 16, (i % 32) < 16)
    i = np.clip(lane - B_NOPE, 0, B_ROPE - 1)
    active = (lane >= B_NOPE) & (lane < B_NOPE + B_ROPE)
    tab_b = build(active, i // 16, (i % 16) % 8, 8, (i % 16) < 8)
    return tab_a, tab_b


def kernel(x_prompt, x_sample, cache_a_k, cache_a_v, cache_b_ckv, cache_b_krope, state_c_fwd, state_c_bwd,
           c, c_ctx, norm_w, w_mod, b_mod, w_in, a_qnorm_w, a_knorm_w, b_kvnorm_w, b_w_uk, b_w_uv,
           c_mu_prev, c_mu_next, c_w0, c_w_up, c_a0, c_a_up, c_k_k, c_k_a, c_r_k, c_lnx_w, c_lnx_b,
           w_oa, w_ob, w_oc, w_out, final_norm_w):
    depth = w_in.shape[0]
    bc, tc, d = x_prompt.shape
    bl, tl, _ = x_sample.shape
    past = cache_a_k.shape[2]
    assert d == D_MODEL and tc % CHUNK == 0 and tl % CHUNK == 0 and tl % GRID_W == 0

    f_idx, b_idx = _column_maps()
    w_all = _gather_cols(w_in, np.concatenate([f_idx, b_idx])).astype(BF16)
    uk_idx = np.full((WIDE,), -1, np.int64)
    for h in range(N_HEADS):
        uk_idx[h * LANES:h * LANES + B_NOPE] = h * B_NOPE + np.arange(B_NOPE)
    wuk = _gather_cols(b_w_uk, uk_idx).astype(BF16)
    wuv = b_w_uv.astype(BF16)
    knw = jnp.tile(a_knorm_w, (1, 2)).reshape(depth, 1, LANES)
    qnw = jnp.tile(a_qnorm_w, (1, 2)).reshape(depth, 1, LANES)
    cnw = b_kvnorm_w.reshape(depth, 1, LANES)
    pad_c = CIN_PAD - C_SHIFT_DIM
    mup = jnp.pad(c_mu_prev, ((0, 0), (0, pad_c))).reshape(depth, 1, CIN_PAD)
    mun = jnp.pad(c_mu_next, ((0, 0), (0, pad_c))).reshape(depth, 1, CIN_PAD)
    w0 = c_w0.reshape(depth, 1, 2 * C_WIDTH)
    a0 = c_a0.reshape(depth, 1, 2 * C_WIDTH)

    def lora_stack(up):
        z = jnp.zeros_like(up[:, 0])
        top = jnp.concatenate([up[:, 0], z], axis=-1)
        bot = jnp.concatenate([z, up[:, 1]], axis=-1)
        return jnp.concatenate([top, bot], axis=-2).astype(BF16)

    wup, aup = lora_stack(c_w_up), lora_stack(c_a_up)
    k_k = c_k_k.reshape(depth, 1, C_WIDTH)
    k_a = c_k_a.reshape(depth, 1, C_WIDTH)
    r_k = c_r_k.reshape(depth, 1, C_WIDTH)
    lnw = c_lnx_w.reshape(depth, 1, C_WIDTH)
    lnb = c_lnx_b.reshape(depth, 1, C_WIDTH)
    woa, wob, woc, wout = (w.astype(BF16) for w in (w_oa, w_ob, w_oc, w_out))
    hid = np.arange(2 * LANES) // HEAD_DIM
    ones_bd = jnp.asarray((hid[:, None] == hid[None, :]).astype(np.float32)).astype(BF16)
    tab_a, tab_b = _rope_tables(tl)

    rows = -(-(bl + 1) // SUBLANES) * SUBLANES
    cond = jnp.zeros((rows, d), F32).at[0:bl].set(c).at[bl].set(c_ctx)
    mod = _modulation(cond, w_mod, b_mod)

    def mod_parts(l, lo, hi):
        m = mod[l, lo:hi]
        return tuple(m[:, j * d:(j + 1) * d].reshape(hi - lo, 1, d) for j in range(3))

    zeros_state = jnp.zeros((bc, N_PAIRS, LANES, LANES), F32)
    cache_kr_pad = jnp.pad(cache_b_krope, ((0, 0), (0, 0), (0, 0), (B_NOPE, LANES - B_NOPE - B_ROPE)))
    cache_kv = jnp.concatenate([cache_a_k.reshape(bl, depth, past, LANES),
                                cache_a_v.reshape(bl, depth, past, LANES),
                                cache_b_ckv, cache_kr_pad], axis=-1)

    xp = x_prompt.reshape(1, bc * tc, d)
    xs = x_sample
    new_ak, new_av, new_ckv, new_kr, new_sf, new_sb = [], [], [], [], [], []

    def mixer_tail(x, gate, ub, uf, ka, va, kb, vb, tabs_q, h0f, h0b, l, nb_tok):
        b_, t_ = ub.shape[0], ub.shape[1]
        ya = _attn(ub, ka, va, tabs_q[0], qnw[l], mode="A", q_off=OFF_AQ, z_off=OFF_AZ)
        yb = _attn(ub, kb, vb, tabs_q[1], None, mode="B", q_off=OFF_BQ, z_off=OFF_BZ)
        prep = _rwkv_prep(uf, mup[l], mun[l], w0[l], a0[l], wup[l], aup[l], k_k[l], k_a[l], r_k[l], ones_bd)
        yf, ybw, hff, hfb = _rwkv_scan(prep, h0f, h0b)
        rs = lambda z: z.reshape(nb_tok, (b_ * t_) // nb_tok, z.shape[-1])
        x_new = _merge(x, gate, rs(ya), rs(yb), rs(yf), rs(ybw), rs(prep[3]), rs(ub),
                       woa[l], wob[l], woc[l], wout[l], lnw[l], lnb[l], ones_bd)
        return x_new, hff, hfb

    for l in range(depth):
        nw = norm_w[l].reshape(1, d)
        shift, scale, gate = mod_parts(l, bl, bl + 1)
        uf, ub = _proj(xp, shift, scale, nw, w_all[l])
        uf, ub = uf.reshape(bc, tc, NF), ub.reshape(bc, tc, NB16)
        ka, va, kb, vb, k_n, ckv_n = _kv_prep(uf, CIN_PAD // KV_W, None, None, knw[l], cnw[l], wuk[l], wuv[l],
                                               emit=True)
        xp, hff, hfb = mixer_tail(xp, gate, ub, uf, ka, va, kb, vb, (None, None), zeros_state, zeros_state, l, 1)
        new_ak.append(k_n.reshape(bc, tc, A_KV_HEADS, HEAD_DIM))
        new_av.append(uf[:, :, CIN_PAD + LANES:CIN_PAD + 2 * LANES].reshape(bc, tc, A_KV_HEADS, HEAD_DIM))
        new_ckv.append(ckv_n)
        new_kr.append(uf[:, :, CIN_PAD + 3 * LANES + B_NOPE:CIN_PAD + 3 * LANES + B_NOPE + B_ROPE])
        new_sf.append(_pairs_to_state(hff))
        new_sb.append(_pairs_to_state(hfb))
        shift, scale, gate = mod_parts(l, 0, bl)
        uf, ub = _proj(xs, shift, scale, nw, w_all[l])
        ka, va, kb, vb = _kv_prep(uf, CIN_PAD // KV_W, cache_kv[:, l], tab_a + tab_b,
                                  knw[l], cnw[l], wuk[l], wuv[l], emit=False)
        xs, _, _ = mixer_tail(xs, gate, ub, uf, ka, va, kb, vb, (tab_a, tab_b),
                              _state_to_pairs(state_c_fwd[:, l]), _state_to_pairs(state_c_bwd[:, l]), l, bl)

    y_prompt = _final_norm(xp, final_norm_w).reshape(bc, tc, d)
    y_sample = _final_norm(xs, final_norm_w)
    return (y_prompt, y_sample,
            jnp.stack(new_ak, axis=1), jnp.stack(new_av, axis=1),
            jnp.stack(new_ckv, axis=1), jnp.stack(new_kr, axis=1),
            jnp.stack(new_sf, axis=1), jnp.stack(new_sb, axis=1))
```

```python
import functools
import math

import numpy as np
import jax
import jax.numpy as jnp
from jax import lax
from jax.experimental import pallas as pl
from jax.experimental.pallas import tpu as pltpu

F32 = jnp.float32
BF16 = jnp.bfloat16

D_MODEL = 1024
GRID_W = 64
HEAD_DIM = 64
N_HEADS = 8
A_KV_HEADS = 2
A_GROUPS = N_HEADS // A_KV_HEADS
B_NOPE = 64
B_ROPE = 32
B_KV_RANK = 128
C_WIDTH = N_HEADS * HEAD_DIM
C_LORA = 64
C_SHIFT_DIM = 3 * C_WIDTH + 4 * C_LORA
ROPE_THETA = 10000.0
NORM_EPS = 1e-6
C_GN_EPS = 64e-5
LOG2_E = math.log2(math.e)

LANES = 128
SUBLANES = 8
VMEM_LIMIT_BYTES = 56 * 1024 * 1024

PROJ_COL_CHUNK = 512
ATTN_Q_TILE = 512
ATTN_PAIRS_PER_STEP = 2
ATTN_SUB_TILE = 256
ATTN_KEY_CHUNK = 256
ATTN_SKEW = 2
VT_ROWS = HEAD_DIM + 16
CHUNK = 64
N_PAIRS = N_HEADS // 2
WIDE = N_HEADS * LANES

CIN_PAD = 2048
KV_W = 4 * LANES
NF = CIN_PAD + KV_W
G_W = 3 * D_MODEL
OFF_G = 0
OFF_AQ = OFF_G + G_W
OFF_AZ = OFF_AQ + WIDE
OFF_BQ = OFF_AZ + C_WIDTH
OFF_BZ = OFF_BQ + WIDE
OFF_CZ = OFF_BZ + C_WIDTH
NB16 = OFF_CZ + C_WIDTH

_R_AQ, _R_AK, _R_AV, _R_AZ = 0, 512, 640, 768
_R_BQ, _R_CKV, _R_KR, _R_BZ = 1280, 2048, 2176, 2208
_R_CIN, _R_CZ, _R_G = 2720, 4512, 5024


def _column_maps():
    f = np.full((NF,), -1, np.int64)
    f[0:C_SHIFT_DIM] = _R_CIN + np.arange(C_SHIFT_DIM)
    f[CIN_PAD:CIN_PAD + 128] = _R_AK + np.arange(128)
    f[CIN_PAD + 128:CIN_PAD + 256] = _R_AV + np.arange(128)
    f[CIN_PAD + 256:CIN_PAD + 384] = _R_CKV + np.arange(128)
    f[CIN_PAD + 384 + 64:CIN_PAD + 384 + 96] = _R_KR + np.arange(32)
    b = np.full((NB16,), -1, np.int64)
    b[OFF_G:OFF_G + G_W] = _R_G + np.arange(G_W)
    for h in range(N_HEADS):
        kv = h // A_GROUPS
        lo = OFF_AQ + h * LANES + kv * HEAD_DIM
        b[lo:lo + HEAD_DIM] = _R_AQ + h * HEAD_DIM + np.arange(HEAD_DIM)
        lo = OFF_BQ + h * LANES
        b[lo:lo + B_NOPE + B_ROPE] = _R_BQ + h * (B_NOPE + B_ROPE) + np.arange(B_NOPE + B_ROPE)
    b[OFF_AZ:OFF_AZ + C_WIDTH] = _R_AZ + np.arange(C_WIDTH)
    b[OFF_BZ:OFF_BZ + C_WIDTH] = _R_BZ + np.arange(C_WIDTH)
    b[OFF_CZ:OFF_CZ + C_WIDTH] = _R_CZ + np.arange(C_WIDTH)
    return f, b


def _gather_cols(w, idx):
    pieces, start = [], 0
    for pos in range(1, len(idx) + 1):
        run_ends = pos == len(idx) or (idx[pos] != idx[pos - 1] + 1 if idx[pos - 1] >= 0 else idx[pos] >= 0)
        if run_ends:
            if idx[start] >= 0:
                pieces.append(w[..., int(idx[start]):int(idx[pos - 1]) + 1])
            else:
                pieces.append(jnp.zeros(w.shape[:-1] + (pos - start,), w.dtype))
            start = pos
    return jnp.concatenate(pieces, axis=-1)


def _tile(n, pref):
    t = min(n, pref)
    assert n % t == 0, (n, pref)
    return t


def _cparams(sem):
    return pltpu.CompilerParams(dimension_semantics=sem, vmem_limit_bytes=VMEM_LIMIT_BYTES)


def _silu(z):
    return z * jax.nn.sigmoid(z)


def _dot(a, b):
    return jnp.dot(a, b, preferred_element_type=F32)


def _dot_nt(a, b):
    return lax.dot_general(a, b, (((1,), (1,)), ((), ())), preferred_element_type=F32)


def _dot_tn(a, b):
    return lax.dot_general(a, b, (((0,), (0,)), ((), ())), preferred_element_type=F32)


def _head_sum(x, ones_bd):
    hi = x.astype(BF16)
    lo = (x - hi.astype(F32)).astype(BF16)
    w = ones_bd.shape[0]
    halves = [_dot(hi[:, c:c + w], ones_bd) + _dot(lo[:, c:c + w], ones_bd) for c in range(0, x.shape[1], w)]
    return jnp.concatenate(halves, axis=1)


def _rope(x, cos, s1, s2, shift):
    n = x.shape[-1]
    return x * cos + pltpu.roll(x, n - shift, 1) * s1 + pltpu.roll(x, shift, 1) * s2


def _mod_kernel(cond_ref, w_ref, b_ref, o_ref):
    s = _silu(cond_ref[...]).astype(BF16)
    o_ref[0] = _dot(s, w_ref[0].astype(BF16)) + b_ref[0]


def _modulation(cond, w_mod, b_mod):
    depth, d, n = w_mod.shape
    rows = cond.shape[0]
    tn = _tile(n, 1024)
    return pl.pallas_call(
        _mod_kernel,
        grid=(depth, n // tn),
        in_specs=[pl.BlockSpec((rows, d), lambda l, j: (0, 0)),
                  pl.BlockSpec((1, d, tn), lambda l, j: (l, 0, j)),
                  pl.BlockSpec((1, 1, tn), lambda l, j: (l, 0, j))],
        out_specs=pl.BlockSpec((1, rows, tn), lambda l, j: (l, 0, j)),
        out_shape=jax.ShapeDtypeStruct((depth, rows, n), F32),
        compiler_params=_cparams(("parallel", "parallel")),
        name="modulation",
    )(cond, w_mod, b_mod.reshape(depth, 1, n))


def _proj_kernel(x_ref, sh_ref, sc_ref, nw_ref, w_ref, of_ref, ob_ref):
    x = x_ref[0]
    ms = jnp.mean(x * x, axis=-1, keepdims=True)
    y = x * lax.rsqrt(ms + NORM_EPS) * nw_ref[...]
    h = (y * (1.0 + sc_ref[0]) + sh_ref[0]).astype(BF16)
    for c0 in range(0, NF, PROJ_COL_CHUNK):
        of_ref[0, :, c0:c0 + PROJ_COL_CHUNK] = _dot(h, w_ref[:, c0:c0 + PROJ_COL_CHUNK])
    for c0 in range(0, NB16, PROJ_COL_CHUNK):
        ob_ref[0, :, c0:c0 + PROJ_COL_CHUNK] = _dot(h, w_ref[:, NF + c0:NF + c0 + PROJ_COL_CHUNK]).astype(BF16)


def _proj(x, shift, scale, norm_w, w):
    nb, t, d = x.shape
    tm = _tile(t, 256)
    return pl.pallas_call(
        _proj_kernel,
        grid=(nb, t // tm),
        in_specs=[pl.BlockSpec((1, tm, d), lambda b, i: (b, i, 0)),
                  pl.BlockSpec((1, 1, d), lambda b, i: (b, 0, 0)),
                  pl.BlockSpec((1, 1, d), lambda b, i: (b, 0, 0)),
                  pl.BlockSpec((1, d), lambda b, i: (0, 0)),
                  pl.BlockSpec((d, NF + NB16), lambda b, i: (0, 0), pipeline_mode=pl.Buffered(1))],
        out_specs=[pl.BlockSpec((1, tm, NF), lambda b, i: (b, i, 0)),
                   pl.BlockSpec((1, tm, NB16), lambda b, i: (b, i, 0))],
        out_shape=[jax.ShapeDtypeStruct((nb, t, NF), F32), jax.ShapeDtypeStruct((nb, t, NB16), BF16)],
        compiler_params=_cparams(("parallel", "parallel")),
        name="proj",
    )(x, shift, scale, norm_w, w)


def _kv_prep_tile(kv, tabs, knw, cnw, wuk, wuv, outs, cache_outs, *, norm):
    ka_ref, va_ref, kb_ref, vb_ref = outs
    ak = kv[:, 0:LANES]
    av = kv[:, LANES:2 * LANES]
    ckv = kv[:, 2 * LANES:3 * LANES]
    kr = kv[:, 3 * LANES:4 * LANES]
    lo = lax.broadcasted_iota(jnp.int32, (1, LANES), 1) < HEAD_DIM
    if norm:
        sq = ak * ak
        s0 = jnp.sum(jnp.where(lo, sq, 0.0), axis=-1, keepdims=True)
        s1 = jnp.sum(jnp.where(lo, 0.0, sq), axis=-1, keepdims=True)
        ms = jnp.where(lo, s0, s1) * (1.0 / HEAD_DIM)
        ak = ak * lax.rsqrt(ms + NORM_EPS) * knw
        ckv = ckv * lax.rsqrt(jnp.mean(ckv * ckv, axis=-1, keepdims=True) + NORM_EPS) * cnw
    if cache_outs is not None:
        cache_outs[0][0] = ak
        cache_outs[1][0] = ckv
    if tabs is not None:
        ca, sa1, sa2, cb, sb1, sb2 = tabs
        ak = _rope(ak, ca, sa1, sa2, 16)
        kr = _rope(kr, cb, sb1, sb2, 8)
    ka_ref[0] = ak.astype(BF16)
    tm = kv.shape[0]
    ones_rows = jnp.ones((VT_ROWS - HEAD_DIM, tm), BF16)

    def put_pair(ref, h0, pair):
        pt = jnp.transpose(pair).astype(BF16)
        for e in range(2):
            ref[0, h0 + e, 0:HEAD_DIM, :] = pt[e * HEAD_DIM:(e + 1) * HEAD_DIM]
            ref[0, h0 + e, HEAD_DIM:VT_ROWS, :] = ones_rows

    put_pair(va_ref, 0, av)
    cb16 = ckv.astype(BF16)
    kn = _dot(cb16, wuk)
    for h in range(N_HEADS):
        kb_ref[0, :, h * LANES:(h + 1) * LANES] = (kn[:, h * LANES:(h + 1) * LANES] + kr).astype(BF16)
    vn = _dot(cb16, wuv)
    for p in range(N_PAIRS):
        put_pair(vb_ref, 2 * p, vn[:, p * LANES:(p + 1) * LANES])


def _kv_prep_kernel(*refs, rope, emit, n_new):
    it = iter(refs)
    kv_ref = next(it)
    cache_ref = next(it) if n_new is not None else None
    tabs = tuple(next(it)[...] for _ in range(6)) if rope else None
    knw, cnw, wuk, wuv = (next(it)[...] for _ in range(4))
    outs = (next(it), next(it), next(it), next(it))
    cache_outs = (next(it), next(it)) if emit else None
    if n_new is None:
        _kv_prep_tile(kv_ref[0], tabs, knw, cnw, wuk, wuv, outs, cache_outs, norm=True)
        return
    i = pl.program_id(1)

    @pl.when(i < n_new)
    def _():
        _kv_prep_tile(kv_ref[0], tabs, knw, cnw, wuk, wuv, outs, cache_outs, norm=True)

    @pl.when(i >= n_new)
    def _():
        _kv_prep_tile(cache_ref[0], None, knw, cnw, wuk, wuv, outs, None, norm=False)


def _kv_prep(kv, col_block, cache, tabs, knw, cnw, wuk, wuv, *, emit):
    b, t = kv.shape[0], kv.shape[1]
    rope = tabs is not None
    if cache is None:
        tm, n_new, total = _tile(t, 512), None, t
        in_specs = [pl.BlockSpec((1, tm, KV_W), lambda bb, i: (bb, i, col_block))]
        args = [kv]
        tab_spec = pl.BlockSpec((tm, LANES), lambda bb, i: (i, 0))
    else:
        past = cache.shape[1]
        tm = _tile(past, 512)
        assert t % tm == 0
        n_new, total = t // tm, t + past
        in_specs = [pl.BlockSpec((1, tm, KV_W), lambda bb, i: (bb, jnp.minimum(i, n_new - 1), col_block)),
                    pl.BlockSpec((1, tm, KV_W), lambda bb, i: (bb, jnp.maximum(i - n_new, 0), 0))]
        args = [kv, cache]
        tab_spec = pl.BlockSpec((tm, LANES), lambda bb, i: (jnp.minimum(i, n_new - 1), 0))
    if rope:
        in_specs += [tab_spec] * 6
        args += list(tabs)
    const = lambda bb, i: (0, 0)
    in_specs += [pl.BlockSpec((1, LANES), const), pl.BlockSpec((1, LANES), const),
                 pl.BlockSpec((B_KV_RANK, WIDE), const), pl.BlockSpec((B_KV_RANK, C_WIDTH), const)]
    args += [knw, cnw, wuk, wuv]

    def rows_out(w, dt):
        return pl.BlockSpec((1, tm, w), lambda bb, i: (bb, i, 0)), jax.ShapeDtypeStruct((b, total, w), dt)

    def vt_out(nh):
        return (pl.BlockSpec((1, nh, VT_ROWS, tm), lambda bb, i: (bb, 0, 0, i)),
                jax.ShapeDtypeStruct((b, nh, VT_ROWS, total), BF16))

    outs = [rows_out(LANES, BF16), vt_out(A_KV_HEADS), rows_out(WIDE, BF16), vt_out(N_HEADS)]
    if emit:
        outs += [rows_out(LANES, F32), rows_out(LANES, F32)]
    out_specs = [o[0] for o in outs]
    out_shape = [o[1] for o in outs]
    return pl.pallas_call(
        functools.partial(_kv_prep_kernel, rope=rope, emit=emit, n_new=n_new),
        grid=(b, total // tm),
        in_specs=in_specs, out_specs=out_specs, out_shape=out_shape,
        compiler_params=_cparams(("parallel", "parallel")),
        name="kv_prep",
    )(*args)


def _attn_kernel(*refs, mode, rope, scale):
    it = iter(refs)
    q_ref, z_ref, k_ref, vt_ref = next(it), next(it), next(it), next(it)
    if rope:
        cos, s1, s2 = (next(it)[...] for _ in range(3))
    if mode == "A":
        qnw = next(it)[...]
    o_ref = next(it)

    tq = q_ref.shape[1]
    tw = min(tq, ATTN_SUB_TILE)
    s_len = k_ref.shape[1]
    ck = ATTN_KEY_CHUNK if s_len % ATTN_KEY_CHUNK == 0 else LANES
    n_chunks = s_len // ck
    n_pairs = q_ref.shape[2] // (2 * LANES)
    streams = [(e, r) for r in range(tq // tw) for e in range(2)]
    qs = {}

    def prep_queries(pr):
        for e, r in streams:
            rs = slice(r * tw, (r + 1) * tw)
            c0 = (2 * pr + e) * LANES
            qh = q_ref[0, rs, c0:c0 + LANES].astype(F32)
            if mode == "A":
                ms = jnp.sum(qh * qh, axis=-1, keepdims=True) * (1.0 / HEAD_DIM)
                qh = qh * lax.rsqrt(ms + NORM_EPS) * qnw
            if rope:
                qh = _rope(qh, cos[rs], s1[rs], s2[rs], 16 if mode == "A" else 8)
            qs[pr, e, r] = (qh * (scale * LOG2_E)).astype(BF16)

    m, acc, st = {}, {}, {}

    def scores(c, t):
        pr, j = divmod(t, n_chunks)
        e = c[0]
        h = 2 * pr + e
        kh = k_ref[0, j * ck:(j + 1) * ck, :] if mode == "A" else k_ref[0, j * ck:(j + 1) * ck, h * LANES:(h + 1) * LANES]
        st[c, t] = _dot_nt(kh, qs[(pr,) + c])

    def consume(c, t):
        pr, j = divmod(t, n_chunks)
        e, r = c
        if j == 0:
            m[c] = jnp.full((1, tw), -1e30, F32)
            acc[c] = jnp.zeros((VT_ROWS, tw), F32)
        s = st.pop((c, t))
        m_new = jnp.maximum(m[c], jnp.max(s, axis=0, keepdims=True))
        p = jnp.exp2(s - m_new).astype(BF16)
        vt = vt_ref[0, 0 if mode == "A" else 2 * pr + e, :, j * ck:(j + 1) * ck]
        acc[c] = acc[c] * jnp.exp2(m[c] - m_new) + _dot(vt, p)
        m[c] = m_new
        if j == n_chunks - 1 and e == 1:
            rs = slice(r * tw, (r + 1) * tw)
            o_t = jnp.concatenate([acc[ee, r][0:HEAD_DIM] / acc[ee, r][HEAD_DIM:HEAD_DIM + 1] for ee in range(2)],
                                  axis=0)
            gate = _silu(z_ref[0, rs, pr * LANES:(pr + 1) * LANES].astype(F32))
            o_ref[0, rs, pr * LANES:(pr + 1) * LANES] = (jnp.transpose(o_t) * gate).astype(o_ref.dtype)

    total = n_pairs * n_chunks
    prep_queries(0)
    for t in range(total + ATTN_SKEW):
        if t < total:
            pr, j = divmod(t, n_chunks)
            if j == n_chunks // 2 and pr + 1 < n_pairs:
                prep_queries(pr + 1)
            for c in streams:
                scores(c, t)
        if t >= ATTN_SKEW:
            for c in streams:
                consume(c, t - ATTN_SKEW)


def _attn(ub, k, v, tabs, qnw, *, mode, q_off, z_off):
    b, tq_all = ub.shape[0], ub.shape[1]
    s_len = k.shape[1]
    tq = _tile(tq_all, ATTN_Q_TILE)
    rope = tabs is not None
    npp = ATTN_PAIRS_PER_STEP
    assert N_PAIRS % npp == 0 and A_GROUPS % (2 * npp) == 0
    qw, zw = 2 * npp * LANES, npp * LANES
    qb0, zb0 = q_off // qw, z_off // zw
    in_specs = [pl.BlockSpec((1, tq, qw), lambda bb, p, i: (bb, i, qb0 + p)),
                pl.BlockSpec((1, tq, zw), lambda bb, p, i: (bb, i, zb0 + p))]
    if mode == "A":
        kv_of = A_GROUPS // (2 * npp)
        in_specs += [pl.BlockSpec((1, s_len, LANES), lambda bb, p, i: (bb, 0, 0)),
                     pl.BlockSpec((1, 1, VT_ROWS, s_len), lambda bb, p, i: (bb, p // kv_of, 0, 0))]
        scale = HEAD_DIM ** -0.5
    else:
        in_specs += [pl.BlockSpec((1, s_len, qw), lambda bb, p, i: (bb, 0, p)),
                     pl.BlockSpec((1, 2 * npp, VT_ROWS, s_len), lambda bb, p, i: (bb, p, 0, 0))]
        scale = (B_NOPE + B_ROPE) ** -0.5
    args = [ub, ub, k, v]
    if rope:
        in_specs += [pl.BlockSpec((tq, LANES), lambda bb, p, i: (i, 0))] * 3
        args += list(tabs)
    if mode == "A":
        in_specs += [pl.BlockSpec((1, LANES), lambda bb, p, i: (0, 0))]
        args += [qnw]
    return pl.pallas_call(
        functools.partial(_attn_kernel, mode=mode, rope=rope, scale=scale),
        grid=(b, N_PAIRS // npp, tq_all // tq),
        in_specs=in_specs,
        out_specs=pl.BlockSpec((1, tq, zw), lambda bb, p, i: (bb, i, p)),
        out_shape=jax.ShapeDtypeStruct((b, tq_all, C_WIDTH), BF16),
        compiler_params=_cparams(("parallel", "parallel", "parallel")),
        name="attn_" + mode,
    )(*args)


def _rwkv_prep_kernel(cin_ref, hp_ref, hn_ref, mup_ref, mun_ref, w0_ref, a0_ref, wup_ref, aup_ref,
                      kk_ref, ka_ref, rk_ref, ones_ref,
                      r_o, v_o, kk_o, bonus_o, lwf_o, kf_o, bf_o, lwb_o, kb_o, bb_o):
    i = pl.program_id(1)
    n = pl.num_programs(1)
    s = cin_ref[0]
    tm = s.shape[0]
    prev_row = jnp.where(i > 0, hp_ref[0, SUBLANES - 1:SUBLANES, :], 0.0)
    next_row = jnp.where(i < n - 1, hn_ref[0, 0:1, :], 0.0)
    rows = lax.broadcasted_iota(jnp.int32, (tm, 1), 0)
    prev = jnp.where(rows == 0, prev_row, pltpu.roll(s, 1, 0))
    nxt = jnp.where(rows == tm - 1, next_row, pltpu.roll(s, tm - 1, 0))
    x = s + mup_ref[...] * (prev - s) + mun_ref[...] * (nxt - s)

    w = C_WIDTH
    r = x[:, 0:w]
    k = x[:, w:2 * w]
    v = x[:, 2 * w:3 * w]
    wd = jnp.tanh(x[:, 3 * w:3 * w + LANES]).astype(BF16)
    ad = x[:, 3 * w + LANES:3 * w + 2 * LANES].astype(BF16)
    w_raw = w0_ref[...] + _dot(wd, wup_ref[...])
    logw = (-math.exp(-0.5)) * jax.nn.sigmoid(w_raw)
    a = jax.nn.sigmoid(a0_ref[...] + _dot(ad, aup_ref[...]))
    ones_bd = ones_ref[...]
    kk = k * kk_ref[...]
    kk = kk / jnp.maximum(jnp.sqrt(_head_sum(kk * kk, ones_bd)), 1e-12)
    ka = ka_ref[...]
    a_f, a_b = a[:, 0:w], a[:, w:2 * w]
    k_f = k * (1.0 + (a_f - 1.0) * ka)
    k_b = k * (1.0 + (a_b - 1.0) * ka)
    r_o[0] = r
    v_o[0] = v
    kk_o[0] = kk
    bonus_o[0] = _head_sum(r * (k_f + k_b) * rk_ref[...], ones_bd) * v
    lwf_o[0] = logw[:, 0:w]
    kf_o[0] = k_f
    bf_o[0] = kk * a_f
    lwb_o[0] = logw[:, w:2 * w]
    kb_o[0] = k_b
    bb_o[0] = kk * a_b


def _rwkv_prep(uf, mup, mun, w0, a0, wup, aup, k_k, k_a, r_k, ones_bd):
    b, t = uf.shape[0], uf.shape[1]
    tm = _tile(t, 256)
    nblk8 = t // SUBLANES
    step8 = tm // SUBLANES
    const = lambda bb, i: (0, 0)
    in_specs = [
        pl.BlockSpec((1, tm, CIN_PAD), lambda bb, i: (bb, i, 0)),
        pl.BlockSpec((1, SUBLANES, CIN_PAD), lambda bb, i: (bb, jnp.maximum(i * step8 - 1, 0), 0)),
        pl.BlockSpec((1, SUBLANES, CIN_PAD), lambda bb, i: (bb, jnp.minimum((i + 1) * step8, nblk8 - 1), 0)),
        pl.BlockSpec((1, CIN_PAD), const), pl.BlockSpec((1, CIN_PAD), const),
        pl.BlockSpec((1, 2 * C_WIDTH), const), pl.BlockSpec((1, 2 * C_WIDTH), const),
        pl.BlockSpec((LANES, 2 * C_WIDTH), const), pl.BlockSpec((LANES, 2 * C_WIDTH), const),
        pl.BlockSpec((1, C_WIDTH), const), pl.BlockSpec((1, C_WIDTH), const), pl.BlockSpec((1, C_WIDTH), const),
        pl.BlockSpec((2 * LANES, 2 * LANES), const),
    ]
    out_specs = [pl.BlockSpec((1, tm, C_WIDTH), lambda bb, i: (bb, i, 0))] * 10
    out_shape = [jax.ShapeDtypeStruct((b, t, C_WIDTH), F32)] * 10
    return pl.pallas_call(
        _rwkv_prep_kernel,
        grid=(b, t // tm),
        in_specs=in_specs, out_specs=out_specs, out_shape=out_shape,
        compiler_params=_cparams(("parallel", "parallel")),
        name="rwkv_prep",
    )(uf, uf, uf, mup, mun, w0, a0, wup, aup, k_k, k_a, r_k, ones_bd)


def _chunk_operands(r, v, kk, logw, kd, bd, rev):
    c = CHUNK
    rows = lax.broadcasted_iota(jnp.int32, (c, 1), 0)
    cum = logw
    shift = 1
    while shift < c:
        if rev:
            cum = cum + jnp.where(rows < c - shift, pltpu.roll(cum, c - shift, 0), 0.0)
        else:
            cum = cum + jnp.where(rows >= shift, pltpu.roll(cum, shift, 0), 0.0)
        shift *= 2
    tot = cum[0:1, :] if rev else cum[c - 1:c, :]
    e_neg = jnp.exp(-cum)
    e_rem = jnp.exp(tot - cum)
    return dict(a=-kk * jnp.exp(cum - logw), r=r * jnp.exp(cum), b=bd * e_neg, k=kd * e_neg,
                bc=bd * e_rem, kc=kd * e_rem, v=v, g_tot=jnp.exp(tot))


def _scan_chunk(dirs, h_scr, y_refs):
    c = CHUNK
    lane = lax.broadcasted_iota(jnp.int32, (1, LANES), 1)
    m_lo = jnp.where(lane < HEAD_DIM, 1.0, 0.0)
    m_hi = 1.0 - m_lo
    row = lax.broadcasted_iota(jnp.int32, (LANES, LANES), 0)
    col = lax.broadcasted_iota(jnp.int32, (LANES, LANES), 1)
    same = (row // c) == (col // c)
    diag = row == col
    eye = jnp.where(diag, 1.0, 0.0)

    chains = []
    for d_idx, (refs, rev) in enumerate(dirs):
        before = (col > row) if rev else (col < row)
        strict = same & before
        incl = same & (before | diag)
        for p in range(N_PAIRS):
            sl = slice(p * LANES, (p + 1) * LANES)
            ops = _chunk_operands(*(ref[0, :, sl] for ref in refs), rev=rev)
            ch = {name: jnp.concatenate([ops[name] * m_lo, ops[name] * m_hi], axis=0)
                  for name in ("a", "r", "b", "k", "bc", "kc", "v")}
            ch.update(d=d_idx, sl=sl, rev=rev, strict=strict, incl=incl, g_tot=ops["g_tot"])
            ar = jnp.concatenate([ch["a"], ch["r"]], axis=0).astype(BF16)
            bk = jnp.concatenate([ch["b"], ch["k"]], axis=0).astype(BF16)
            ch["sc"] = _dot_nt(ar, bk)
            chains.append(ch)

    for ch in chains:
        sc = ch.pop("sc")
        ch["a_ab"] = jnp.where(ch["strict"], sc[0:LANES, 0:LANES], 0.0)
        a_ak = jnp.where(ch["strict"], sc[0:LANES, LANES:2 * LANES], 0.0)
        ch["m_rb"] = jnp.where(ch["incl"], sc[LANES:2 * LANES, 0:LANES], 0.0).astype(BF16)
        m_rk = jnp.where(ch["incl"], sc[LANES:2 * LANES, LANES:2 * LANES], 0.0)
        ch["vb"] = ch["v"].astype(BF16)
        xm = _dot(jnp.concatenate([a_ak, m_rk], axis=0).astype(BF16), ch["vb"])
        ch["x"] = xm[0:LANES]
        ch["mv"] = xm[LANES:2 * LANES]
        ch["kv"] = _dot_tn(ch["kc"].astype(BF16), ch["vb"])
    s_blk = 1
    while s_blk < c:
        for ch in chains:
            half = (row // s_blk) % 2
            if ch["rev"]:
                off = (half == 0) & ((col // s_blk) == (row // s_blk) + 1)
            else:
                off = (half == 1) & ((col // s_blk) == (row // s_blk) - 1)
            a_off = jnp.where(off, ch["a_ab"], 0.0)
            if s_blk == 1:
                ch["t"] = eye + a_off
            else:
                ch["x_off"] = _dot(a_off.astype(BF16), ch["t"].astype(BF16))
        if s_blk > 1:
            for ch in chains:
                ch["t"] = ch["t"] + _dot(ch["t"].astype(BF16), ch.pop("x_off").astype(BF16))
        s_blk *= 2
    for ch in chains:
        ax = jnp.concatenate([ch["a"], ch["x"]], axis=1).astype(BF16)
        ch["g"] = _dot(ch["t"].astype(BF16), ax).astype(BF16)
    for ch in chains:
        ch["ry"] = jnp.concatenate([ch["r"], ch["mv"]], axis=1) + _dot(ch["m_rb"], ch["g"])
        ch["pq"] = _dot_tn(ch["bc"].astype(BF16), ch["g"])
    for ch in chains:
        q_mat = ch["pq"][:, LANES:2 * LANES] + ch["kv"]
        d_idx, p = ch["d"], ch["sl"].start // LANES
        h_old = h_scr[d_idx, p]
        rp = jnp.concatenate([ch["ry"][:, 0:LANES], ch["pq"][:, 0:LANES]], axis=0).astype(BF16)
        yh = _dot(rp, h_old.astype(BF16))
        y2 = ch["ry"][:, LANES:2 * LANES] + yh[0:LANES]
        g_rows = jnp.transpose(jnp.broadcast_to(ch["g_tot"], (LANES, LANES)))
        h_scr[d_idx, p] = g_rows * h_old + yh[LANES:2 * LANES] + q_mat
        y_refs[d_idx][0, :, ch["sl"]] = y2[0:c, :] + y2[c:2 * c, :]


def _rwkv_scan_kernel(rf, vf, kkf, lwf, kf, bf, rb, vb, kkb, lwb, kb, bb, h0f, h0b,
                      yf, yb, hff, hfb, h_scr):
    i = pl.program_id(1)

    @pl.when(i == 0)
    def _():
        h_scr[0] = h0f[0]
        h_scr[1] = h0b[0]

    _scan_chunk((((rf, vf, kkf, lwf, kf, bf), False), ((rb, vb, kkb, lwb, kb, bb), True)), h_scr, (yf, yb))

    @pl.when(i == pl.num_programs(1) - 1)
    def _():
        hff[0] = h_scr[0]
        hfb[0] = h_scr[1]


def _rwkv_scan(prep, h0f, h0b):
    r, v, kk, _, lwf, kf, bf, lwb, kb, bb = prep
    b, t = r.shape[0], r.shape[1]
    nc = t // CHUNK
    fwd = pl.BlockSpec((1, CHUNK, C_WIDTH), lambda bb_, i: (bb_, i, 0))
    bwd = pl.BlockSpec((1, CHUNK, C_WIDTH), lambda bb_, i: (bb_, nc - 1 - i, 0))
    st = pl.BlockSpec((1, N_PAIRS, LANES, LANES), lambda bb_, i: (bb_, 0, 0, 0))
    return pl.pallas_call(
        _rwkv_scan_kernel,
        grid=(b, nc),
        in_specs=[fwd] * 6 + [bwd] * 6 + [st, st],
        out_specs=[fwd, bwd, st, st],
        out_shape=[jax.ShapeDtypeStruct((b, t, C_WIDTH), F32)] * 2
        + [jax.ShapeDtypeStruct((b, N_PAIRS, LANES, LANES), F32)] * 2,
        scratch_shapes=[pltpu.VMEM((2, N_PAIRS, LANES, LANES), F32)],
        compiler_params=_cparams(("parallel", "arbitrary")),
        name="rwkv_scan",
    )(r, v, kk, lwf, kf, bf, r, v, kk, lwb, kb, bb, h0f, h0b)


def _state_to_pairs(s):
    h = jnp.swapaxes(s.astype(F32), -1, -2).reshape(s.shape[0], N_PAIRS, 2, HEAD_DIM, HEAD_DIM)
    z = jnp.zeros_like(h[:, :, 0])
    top = jnp.concatenate([h[:, :, 0], z], axis=-1)
    bot = jnp.concatenate([z, h[:, :, 1]], axis=-1)
    return jnp.concatenate([top, bot], axis=-2)


def _pairs_to_state(hp):
    e = hp[:, :, 0:HEAD_DIM, 0:HEAD_DIM]
    o = hp[:, :, HEAD_DIM:, HEAD_DIM:]
    h = jnp.stack([e, o], axis=2).reshape(hp.shape[0], N_HEADS, HEAD_DIM, HEAD_DIM)
    return jnp.swapaxes(h, -1, -2)


def _merge_kernel(x_ref, gate_ref, ya_ref, yb_ref, yf_ref, ybw_ref, bonus_ref, cz_ref, g_ref,
                  woa_ref, wob_ref, woc_ref, wout_ref, lnw_ref, lnb_ref, ones_ref, o_ref):
    ones_bd = ones_ref[...]
    y = yf_ref[0] + ybw_ref[0]
    mu = _head_sum(y, ones_bd) * (1.0 / HEAD_DIM)
    yc = y - mu
    var = _head_sum(yc * yc, ones_bd) * (1.0 / HEAD_DIM)
    yn = yc * lax.rsqrt(var + C_GN_EPS) * lnw_ref[...] + lnb_ref[...] + bonus_ref[0]
    ycg = (yn * _silu(cz_ref[0].astype(F32))).astype(BF16)
    d = D_MODEL
    ga = jax.nn.sigmoid(g_ref[0, :, 0:d].astype(F32))
    gb = jax.nn.sigmoid(g_ref[0, :, d:2 * d].astype(F32))
    gc = jax.nn.sigmoid(g_ref[0, :, 2 * d:3 * d].astype(F32))
    mixed = (ga * _dot(ya_ref[0], woa_ref[...]) + gb * _dot(yb_ref[0], wob_ref[...])
             + gc * _dot(ycg, woc_ref[...]))
    out = _dot(mixed.astype(BF16), wout_ref[...])
    o_ref[0] = x_ref[0] + gate_ref[0] * out


def _merge(x, gate, ya, yb, yf, ybw, bonus, ub, woa, wob, woc, wout, lnw, lnb, ones_bd):
    nb, t, d = x.shape
    tm = _tile(t, 256)
    tok = lambda w: pl.BlockSpec((1, tm, w), lambda b, i: (b, i, 0))
    const = lambda b, i: (0, 0)
    in_specs = [tok(d), pl.BlockSpec((1, 1, d), lambda b, i: (b, 0, 0)),
                tok(C_WIDTH), tok(C_WIDTH), tok(C_WIDTH), tok(C_WIDTH), tok(C_WIDTH),
                pl.BlockSpec((1, tm, C_WIDTH), lambda b, i: (b, i, OFF_CZ // C_WIDTH)),
                pl.BlockSpec((1, tm, G_W), lambda b, i: (b, i, OFF_G // G_W)),
                pl.BlockSpec((C_WIDTH, d), const), pl.BlockSpec((C_WIDTH, d), const),
                pl.BlockSpec((C_WIDTH, d), const), pl.BlockSpec((d, d), const),
                pl.BlockSpec((1, C_WIDTH), const), pl.BlockSpec((1, C_WIDTH), const),
                pl.BlockSpec((2 * LANES, 2 * LANES), const)]
    return pl.pallas_call(
        _merge_kernel,
        grid=(nb, t // tm),
        in_specs=in_specs,
        out_specs=tok(d),
        out_shape=jax.ShapeDtypeStruct((nb, t, d), F32),
        compiler_params=_cparams(("parallel", "parallel")),
        name="merge",
    )(x, gate, ya, yb, yf, ybw, bonus, ub, ub, woa, wob, woc, wout, lnw, lnb, ones_bd)


def _final_norm_kernel(x_ref, w_ref, o_ref):
    x = x_ref[...]
    o_ref[...] = x * lax.rsqrt(jnp.mean(x * x, axis=-1, keepdims=True) + NORM_EPS) * w_ref[...]


def _final_norm(x, w):
    shape = x.shape
    x2 = x.reshape(-1, shape[-1])
    n, d = x2.shape
    tm = _tile(n, 1024)
    out = pl.pallas_call(
        _final_norm_kernel,
        grid=(n // tm,),
        in_specs=[pl.BlockSpec((tm, d), lambda i: (i, 0)), pl.BlockSpec((1, d), lambda i: (0, 0))],
        out_specs=pl.BlockSpec((tm, d), lambda i: (i, 0)),
        out_shape=jax.ShapeDtypeStruct((n, d), F32),
        compiler_params=_cparams(("parallel",)),
        name="final_norm",
    )(x2, w.reshape(1, d))
    return out.reshape(shape)


def _rope_tables(n_tokens):
    t = np.arange(n_tokens)
    pos = np.stack([t // GRID_W, t % GRID_W], axis=0).astype(np.float32)
    lane = np.arange(LANES)

    def build(active, part, freq_idx, half, first):
        inv = jnp.asarray(ROPE_THETA, F32) ** (-jnp.asarray(freq_idx, F32) / half)
        posm = jnp.asarray(pos)[jnp.asarray(part)]
        ang = posm.T * inv[None, :]
        cos = jnp.where(jnp.asarray(active)[None, :], jnp.cos(ang), 1.0)
        sin = jnp.where(jnp.asarray(active)[None, :], jnp.sin(ang), 0.0)
        s1 = jnp.where(jnp.asarray(first)[None, :], -sin, 0.0)
        s2 = jnp.where(jnp.asarray(first)[None, :], 0.0, sin)
        return cos.astype(F32), s1.astype(F32), s2.astype(F32)

    i = lane % HEAD_DIM
    tab_a = build(np.ones(LANES, bool), i // 32, (i % 32) % 16, 16, (i % 32) < 16)
    i = np.clip(lane - B_NOPE, 0, B_ROPE - 1)
    active = (lane >= B_NOPE) & (lane < B_NOPE + B_ROPE)
    tab_b = build(active, i // 16, (i % 16) % 8, 8, (i % 16) < 8)
    return tab_a, tab_b


def kernel(x_prompt, x_sample, cache_a_k, cache_a_v, cache_b_ckv, cache_b_krope, state_c_fwd, state_c_bwd,
           c, c_ctx, norm_w, w_mod, b_mod, w_in, a_qnorm_w, a_knorm_w, b_kvnorm_w, b_w_uk, b_w_uv,
           c_mu_prev, c_mu_next, c_w0, c_w_up, c_a0, c_a_up, c_k_k, c_k_a, c_r_k, c_lnx_w, c_lnx_b,
           w_oa, w_ob, w_oc, w_out, final_norm_w):
    depth = w_in.shape[0]
    bc, tc, d = x_prompt.shape
    bl, tl, _ = x_sample.shape
    past = cache_a_k.shape[2]
    assert d == D_MODEL and tc % CHUNK == 0 and tl % CHUNK == 0 and tl % GRID_W == 0

    f_idx, b_idx = _column_maps()
    w_all = _gather_cols(w_in, np.concatenate([f_idx, b_idx])).astype(BF16)
    uk_idx = np.full((WIDE,), -1, np.int64)
    for h in range(N_HEADS):
        uk_idx[h * LANES:h * LANES + B_NOPE] = h * B_NOPE + np.arange(B_NOPE)
    wuk = _gather_cols(b_w_uk, uk_idx).astype(BF16)
    wuv = b_w_uv.astype(BF16)
    knw = jnp.tile(a_knorm_w, (1, 2)).reshape(depth, 1, LANES)
    qnw = jnp.tile(a_qnorm_w, (1, 2)).reshape(depth, 1, LANES)
    cnw = b_kvnorm_w.reshape(depth, 1, LANES)
    pad_c = CIN_PAD - C_SHIFT_DIM
    mup = jnp.pad(c_mu_prev, ((0, 0), (0, pad_c))).reshape(depth, 1, CIN_PAD)
    mun = jnp.pad(c_mu_next, ((0, 0), (0, pad_c))).reshape(depth, 1, CIN_PAD)
    w0 = c_w0.reshape(depth, 1, 2 * C_WIDTH)
    a0 = c_a0.reshape(depth, 1, 2 * C_WIDTH)

    def lora_stack(up):
        z = jnp.zeros_like(up[:, 0])
        top = jnp.concatenate([up[:, 0], z], axis=-1)
        bot = jnp.concatenate([z, up[:, 1]], axis=-1)
        return jnp.concatenate([top, bot], axis=-2).astype(BF16)

    wup, aup = lora_stack(c_w_up), lora_stack(c_a_up)
    k_k = c_k_k.reshape(depth, 1, C_WIDTH)
    k_a = c_k_a.reshape(depth, 1, C_WIDTH)
    r_k = c_r_k.reshape(depth, 1, C_WIDTH)
    lnw = c_lnx_w.reshape(depth, 1, C_WIDTH)
    lnb = c_lnx_b.reshape(depth, 1, C_WIDTH)
    woa, wob, woc, wout = (w.astype(BF16) for w in (w_oa, w_ob, w_oc, w_out))
    hid = np.arange(2 * LANES) // HEAD_DIM
    ones_bd = jnp.asarray((hid[:, None] == hid[None, :]).astype(np.float32)).astype(BF16)
    tab_a, tab_b = _rope_tables(tl)

    rows = -(-(bl + 1) // SUBLANES) * SUBLANES
    cond = jnp.zeros((rows, d), F32).at[0:bl].set(c).at[bl].set(c_ctx)
    mod = _modulation(cond, w_mod, b_mod)

    def mod_parts(l, lo, hi):
        m = mod[l, lo:hi]
        return tuple(m[:, j * d:(j + 1) * d].reshape(hi - lo, 1, d) for j in range(3))

    zeros_state = jnp.zeros((bc, N_PAIRS, LANES, LANES), F32)
    cache_kr_pad = jnp.pad(cache_b_krope, ((0, 0), (0, 0), (0, 0), (B_NOPE, LANES - B_NOPE - B_ROPE)))
    cache_kv = jnp.concatenate([cache_a_k.reshape(bl, depth, past, LANES),
                                cache_a_v.reshape(bl, depth, past, LANES),
                                cache_b_ckv, cache_kr_pad], axis=-1)

    xp = x_prompt.reshape(1, bc * tc, d)
    xs = x_sample
    new_ak, new_av, new_ckv, new_kr, new_sf, new_sb = [], [], [], [], [], []

    def mixer_tail(x, gate, ub, uf, ka, va, kb, vb, tabs_q, h0f, h0b, l, nb_tok):
        b_, t_ = ub.shape[0], ub.shape[1]
        ya = _attn(ub, ka, va, tabs_q[0], qnw[l], mode="A", q_off=OFF_AQ, z_off=OFF_AZ)
        yb = _attn(ub, kb, vb, tabs_q[1], None, mode="B", q_off=OFF_BQ, z_off=OFF_BZ)
        prep = _rwkv_prep(uf, mup[l], mun[l], w0[l], a0[l], wup[l], aup[l], k_k[l], k_a[l], r_k[l], ones_bd)
        yf, ybw, hff, hfb = _rwkv_scan(prep, h0f, h0b)
        rs = lambda z: z.reshape(nb_tok, (b_ * t_) // nb_tok, z.shape[-1])
        x_new = _merge(x, gate, rs(ya), rs(yb), rs(yf), rs(ybw), rs(prep[3]), rs(ub),
                       woa[l], wob[l], woc[l], wout[l], lnw[l], lnb[l], ones_bd)
        return x_new, hff, hfb

    for l in range(depth):
        nw = norm_w[l].reshape(1, d)
        shift, scale, gate = mod_parts(l, bl, bl + 1)
        uf, ub = _proj(xp, shift, scale, nw, w_all[l])
        uf, ub = uf.reshape(bc, tc, NF), ub.reshape(bc, tc, NB16)
        ka, va, kb, vb, k_n, ckv_n = _kv_prep(uf, CIN_PAD // KV_W, None, None, knw[l], cnw[l], wuk[l], wuv[l],
                                               emit=True)
        xp, hff, hfb = mixer_tail(xp, gate, ub, uf, ka, va, kb, vb, (None, None), zeros_state, zeros_state, l, 1)
        new_ak.append(k_n.reshape(bc, tc, A_KV_HEADS, HEAD_DIM))
        new_av.append(uf[:, :, CIN_PAD + LANES:CIN_PAD + 2 * LANES].reshape(bc, tc, A_KV_HEADS, HEAD_DIM))
        new_ckv.append(ckv_n)
        new_kr.append(uf[:, :, CIN_PAD + 3 * LANES + B_NOPE:CIN_PAD + 3 * LANES + B_NOPE + B_ROPE])
        new_sf.append(_pairs_to_state(hff))
        new_sb.append(_pairs_to_state(hfb))
        shift, scale, gate = mod_parts(l, 0, bl)
        uf, ub = _proj(xs, shift, scale, nw, w_all[l])
        ka, va, kb, vb = _kv_prep(uf, CIN_PAD // KV_W, cache_kv[:, l], tab_a + tab_b,
                                  knw[l], cnw[l], wuk[l], wuv[l], emit=False)
        xs, _, _ = mixer_tail(xs, gate, ub, uf, ka, va, kb, vb, (tab_a, tab_b),
                              _state_to_pairs(state_c_fwd[:, l]), _state_to_pairs(state_c_bwd[:, l]), l, bl)

    y_prompt = _final_norm(xp, final_norm_w).reshape(bc, tc, d)
    y_sample = _final_norm(xs, final_norm_w)
    return (y_prompt, y_sample,
            jnp.stack(new_ak, axis=1), jnp.stack(new_av, axis=1),
            jnp.stack(new_ckv, axis=1), jnp.stack(new_kr, axis=1),
            jnp.stack(new_sf, axis=1), jnp.stack(new_sb, axis=1))
```

```python
import functools
import math

import numpy as np
import jax
import jax.numpy as jnp
from jax import lax
from jax.experimental import pallas as pl
from jax.experimental.pallas import tpu as pltpu

F32 = jnp.float32
BF16 = jnp.bfloat16

D_MODEL = 1024
GRID_W = 64
HEAD_DIM = 64
N_HEADS = 8
A_KV_HEADS = 2
A_GROUPS = N_HEADS // A_KV_HEADS
B_NOPE = 64
B_ROPE = 32
B_KV_RANK = 128
C_WIDTH = N_HEADS * HEAD_DIM
C_LORA = 64
C_SHIFT_DIM = 3 * C_WIDTH + 4 * C_LORA
ROPE_THETA = 10000.0
NORM_EPS = 1e-6
C_GN_EPS = 64e-5
LOG2_E = math.log2(math.e)

LANES = 128
SUBLANES = 8
VMEM_LIMIT_BYTES = 56 * 1024 * 1024

ROW_TILE = 512
PROJ_COL_CHUNK = 512
ATTN_Q_TILE = 512
ATTN_PAIRS_PER_STEP = 2
ATTN_SUB_TILE = 256
ATTN_KEY_CHUNK = 256
ATTN_SKEW = 2
VT_ROWS = HEAD_DIM + 16
CHUNK = 64
N_PAIRS = N_HEADS // 2
WIDE = N_HEADS * LANES

CIN_PAD = 2048
KV_W = 4 * LANES
NF = CIN_PAD + KV_W
G_W = 3 * D_MODEL
OFF_G = 0
OFF_AQ = OFF_G + G_W
OFF_AZ = OFF_AQ + WIDE
OFF_BQ = OFF_AZ + C_WIDTH
OFF_BZ = OFF_BQ + WIDE
OFF_CZ = OFF_BZ + C_WIDTH
NB16 = OFF_CZ + C_WIDTH

_R_AQ, _R_AK, _R_AV, _R_AZ = 0, 512, 640, 768
_R_BQ, _R_CKV, _R_KR, _R_BZ = 1280, 2048, 2176, 2208
_R_CIN, _R_CZ, _R_G = 2720, 4512, 5024


def _column_maps():
    f = np.full((NF,), -1, np.int64)
    f[0:C_SHIFT_DIM] = _R_CIN + np.arange(C_SHIFT_DIM)
    f[CIN_PAD:CIN_PAD + 128] = _R_AK + np.arange(128)
    f[CIN_PAD + 128:CIN_PAD + 256] = _R_AV + np.arange(128)
    f[CIN_PAD + 256:CIN_PAD + 384] = _R_CKV + np.arange(128)
    f[CIN_PAD + 384 + 64:CIN_PAD + 384 + 96] = _R_KR + np.arange(32)
    b = np.full((NB16,), -1, np.int64)
    b[OFF_G:OFF_G + G_W] = _R_G + np.arange(G_W)
    for h in range(N_HEADS):
        kv = h // A_GROUPS
        lo = OFF_AQ + h * LANES + kv * HEAD_DIM
        b[lo:lo + HEAD_DIM] = _R_AQ + h * HEAD_DIM + np.arange(HEAD_DIM)
        lo = OFF_BQ + h * LANES
        b[lo:lo + B_NOPE + B_ROPE] = _R_BQ + h * (B_NOPE + B_ROPE) + np.arange(B_NOPE + B_ROPE)
    b[OFF_AZ:OFF_AZ + C_WIDTH] = _R_AZ + np.arange(C_WIDTH)
    b[OFF_BZ:OFF_BZ + C_WIDTH] = _R_BZ + np.arange(C_WIDTH)
    b[OFF_CZ:OFF_CZ + C_WIDTH] = _R_CZ + np.arange(C_WIDTH)
    return f, b


def _gather_cols(w, idx):
    pieces, start = [], 0
    for pos in range(1, len(idx) + 1):
        run_ends = pos == len(idx) or (idx[pos] != idx[pos - 1] + 1 if idx[pos - 1] >= 0 else idx[pos] >= 0)
        if run_ends:
            if idx[start] >= 0:
                pieces.append(w[..., int(idx[start]):int(idx[pos - 1]) + 1])
            else:
                pieces.append(jnp.zeros(w.shape[:-1] + (pos - start,), w.dtype))
            start = pos
    return jnp.concatenate(pieces, axis=-1)


def _tile(n, pref):
    t = min(n, pref)
    assert n % t == 0, (n, pref)
    return t


def _cparams(sem):
    return pltpu.CompilerParams(dimension_semantics=sem, vmem_limit_bytes=VMEM_LIMIT_BYTES)


def _silu(z):
    return z * jax.nn.sigmoid(z)


def _dot(a, b):
    return jnp.dot(a, b, preferred_element_type=F32)


def _dot_nt(a, b):
    return lax.dot_general(a, b, (((1,), (1,)), ((), ())), preferred_element_type=F32)


def _dot_tn(a, b):
    return lax.dot_general(a, b, (((0,), (0,)), ((), ())), preferred_element_type=F32)


def _head_sum(x, ones_bd):
    hi = x.astype(BF16)
    lo = (x - hi.astype(F32)).astype(BF16)
    w = ones_bd.shape[0]
    halves = [_dot(hi[:, c:c + w], ones_bd) + _dot(lo[:, c:c + w], ones_bd) for c in range(0, x.shape[1], w)]
    return jnp.concatenate(halves, axis=1)


def _rope(x, cos, s1, s2, shift):
    n = x.shape[-1]
    return x * cos + pltpu.roll(x, n - shift, 1) * s1 + pltpu.roll(x, shift, 1) * s2


def _mod_kernel(cond_ref, w_ref, b_ref, o_ref):
    s = _silu(cond_ref[...]).astype(BF16)
    o_ref[0] = _dot(s, w_ref[0].astype(BF16)) + b_ref[0]


def _modulation(cond, w_mod, b_mod):
    depth, d, n = w_mod.shape
    rows = cond.shape[0]
    tn = _tile(n, 1024)
    return pl.pallas_call(
        _mod_kernel,
        grid=(depth, n // tn),
        in_specs=[pl.BlockSpec((rows, d), lambda l, j: (0, 0)),
                  pl.BlockSpec((1, d, tn), lambda l, j: (l, 0, j)),
                  pl.BlockSpec((1, 1, tn), lambda l, j: (l, 0, j))],
        out_specs=pl.BlockSpec((1, rows, tn), lambda l, j: (l, 0, j)),
        out_shape=jax.ShapeDtypeStruct((depth, rows, n), F32),
        compiler_params=_cparams(("parallel", "parallel")),
        name="modulation",
    )(cond, w_mod, b_mod.reshape(depth, 1, n))


def _proj_kernel(x_ref, sh_ref, sc_ref, nw_ref, w_ref, of_ref, ob_ref):
    x = x_ref[0]
    ms = jnp.mean(x * x, axis=-1, keepdims=True)
    y = x * lax.rsqrt(ms + NORM_EPS) * nw_ref[...]
    h = (y * (1.0 + sc_ref[0]) + sh_ref[0]).astype(BF16)
    for c0 in range(0, NF, PROJ_COL_CHUNK):
        of_ref[0, :, c0:c0 + PROJ_COL_CHUNK] = _dot(h, w_ref[:, c0:c0 + PROJ_COL_CHUNK])
    for c0 in range(0, NB16, PROJ_COL_CHUNK):
        ob_ref[0, :, c0:c0 + PROJ_COL_CHUNK] = _dot(h, w_ref[:, NF + c0:NF + c0 + PROJ_COL_CHUNK]).astype(BF16)


def _proj(x, shift, scale, norm_w, w):
    nb, t, d = x.shape
    tm = _tile(t, 256)
    return pl.pallas_call(
        _proj_kernel,
        grid=(nb, t // tm),
        in_specs=[pl.BlockSpec((1, tm, d), lambda b, i: (b, i, 0)),
                  pl.BlockSpec((1, 1, d), lambda b, i: (b, 0, 0)),
                  pl.BlockSpec((1, 1, d), lambda b, i: (b, 0, 0)),
                  pl.BlockSpec((1, d), lambda b, i: (0, 0)),
                  pl.BlockSpec((d, NF + NB16), lambda b, i: (0, 0), pipeline_mode=pl.Buffered(1))],
        out_specs=[pl.BlockSpec((1, tm, NF), lambda b, i: (b, i, 0)),
                   pl.BlockSpec((1, tm, NB16), lambda b, i: (b, i, 0))],
        out_shape=[jax.ShapeDtypeStruct((nb, t, NF), F32), jax.ShapeDtypeStruct((nb, t, NB16), BF16)],
        compiler_params=_cparams(("parallel", "parallel")),
        name="proj",
    )(x, shift, scale, norm_w, w)


def _kv_prep_tile(kv, tabs, knw, cnw, wuk, wuv, outs, cache_outs, *, norm):
    ka_ref, va_ref, kb_ref, vb_ref = outs
    ak = kv[:, 0:LANES]
    av = kv[:, LANES:2 * LANES]
    ckv = kv[:, 2 * LANES:3 * LANES]
    kr = kv[:, 3 * LANES:4 * LANES]
    lo = lax.broadcasted_iota(jnp.int32, (1, LANES), 1) < HEAD_DIM
    if norm:
        sq = ak * ak
        s0 = jnp.sum(jnp.where(lo, sq, 0.0), axis=-1, keepdims=True)
        s1 = jnp.sum(jnp.where(lo, 0.0, sq), axis=-1, keepdims=True)
        ms = jnp.where(lo, s0, s1) * (1.0 / HEAD_DIM)
        ak = ak * lax.rsqrt(ms + NORM_EPS) * knw
        ckv = ckv * lax.rsqrt(jnp.mean(ckv * ckv, axis=-1, keepdims=True) + NORM_EPS) * cnw
    if cache_outs is not None:
        cache_outs[0][0] = ak
        cache_outs[1][0] = ckv
    if tabs is not None:
        ca, sa1, sa2, cb, sb1, sb2 = tabs
        ak = _rope(ak, ca, sa1, sa2, 16)
        kr = _rope(kr, cb, sb1, sb2, 8)
    ka_ref[0] = ak.astype(BF16)
    tm = kv.shape[0]
    ones_rows = jnp.ones((VT_ROWS - HEAD_DIM, tm), BF16)

    def put_pair(ref, h0, pair):
        pt = jnp.transpose(pair).astype(BF16)
        for e in range(2):
            ref[0, h0 + e, 0:HEAD_DIM, :] = pt[e * HEAD_DIM:(e + 1) * HEAD_DIM]
            ref[0, h0 + e, HEAD_DIM:VT_ROWS, :] = ones_rows

    put_pair(va_ref, 0, av)
    cb16 = ckv.astype(BF16)
    kn = _dot(cb16, wuk)
    for h in range(N_HEADS):
        kb_ref[0, :, h * LANES:(h + 1) * LANES] = (kn[:, h * LANES:(h + 1) * LANES] + kr).astype(BF16)
    vn = _dot(cb16, wuv)
    for p in range(N_PAIRS):
        put_pair(vb_ref, 2 * p, vn[:, p * LANES:(p + 1) * LANES])


def _kv_prep_kernel(*refs, rope, emit, n_new):
    it = iter(refs)
    kv_ref = next(it)
    cache_ref = next(it) if n_new is not None else None
    tabs = tuple(next(it)[...] for _ in range(6)) if rope else None
    knw, cnw, wuk, wuv = (next(it)[...] for _ in range(4))
    outs = (next(it), next(it), next(it), next(it))
    cache_outs = (next(it), next(it)) if emit else None
    if n_new is None:
        _kv_prep_tile(kv_ref[0], tabs, knw, cnw, wuk, wuv, outs, cache_outs, norm=True)
        return
    i = pl.program_id(1)

    @pl.when(i < n_new)
    def _():
        _kv_prep_tile(kv_ref[0], tabs, knw, cnw, wuk, wuv, outs, cache_outs, norm=True)

    @pl.when(i >= n_new)
    def _():
        _kv_prep_tile(cache_ref[0], None, knw, cnw, wuk, wuv, outs, None, norm=False)


def _kv_prep(kv, col_block, cache, tabs, knw, cnw, wuk, wuv, *, emit):
    b, t = kv.shape[0], kv.shape[1]
    rope = tabs is not None
    if cache is None:
        tm, n_new, total = _tile(t, 512), None, t
        in_specs = [pl.BlockSpec((1, tm, KV_W), lambda bb, i: (bb, i, col_block))]
        args = [kv]
        tab_spec = pl.BlockSpec((tm, LANES), lambda bb, i: (i, 0))
    else:
        past = cache.shape[1]
        tm = _tile(past, 512)
        assert t % tm == 0
        n_new, total = t // tm, t + past
        in_specs = [pl.BlockSpec((1, tm, KV_W), lambda bb, i: (bb, jnp.minimum(i, n_new - 1), col_block)),
                    pl.BlockSpec((1, tm, KV_W), lambda bb, i: (bb, jnp.maximum(i - n_new, 0), 0))]
        args = [kv, cache]
        tab_spec = pl.BlockSpec((tm, LANES), lambda bb, i: (jnp.minimum(i, n_new - 1), 0))
    if rope:
        in_specs += [tab_spec] * 6
        args += list(tabs)
    const = lambda bb, i: (0, 0)
    in_specs += [pl.BlockSpec((1, LANES), const), pl.BlockSpec((1, LANES), const),
                 pl.BlockSpec((B_KV_RANK, WIDE), const), pl.BlockSpec((B_KV_RANK, C_WIDTH), const)]
    args += [knw, cnw, wuk, wuv]

    def rows_out(w, dt):
        return pl.BlockSpec((1, tm, w), lambda bb, i: (bb, i, 0)), jax.ShapeDtypeStruct((b, total, w), dt)

    def vt_out(nh):
        return (pl.BlockSpec((1, nh, VT_ROWS, tm), lambda bb, i: (bb, 0, 0, i)),
                jax.ShapeDtypeStruct((b, nh, VT_ROWS, total), BF16))

    outs = [rows_out(LANES, BF16), vt_out(A_KV_HEADS), rows_out(WIDE, BF16), vt_out(N_HEADS)]
    if emit:
        outs += [rows_out(LANES, F32), rows_out(LANES, F32)]
    out_specs = [o[0] for o in outs]
    out_shape = [o[1] for o in outs]
    return pl.pallas_call(
        functools.partial(_kv_prep_kernel, rope=rope, emit=emit, n_new=n_new),
        grid=(b, total // tm),
        in_specs=in_specs, out_specs=out_specs, out_shape=out_shape,
        compiler_params=_cparams(("parallel", "parallel")),
        name="kv_prep",
    )(*args)


def _attn_kernel(*refs, mode, rope, scale):
    it = iter(refs)
    q_ref, z_ref, k_ref, vt_ref = next(it), next(it), next(it), next(it)
    if rope:
        cos, s1, s2 = (next(it)[...] for _ in range(3))
    if mode == "A":
        qnw = next(it)[...]
    o_ref = next(it)

    tq = q_ref.shape[1]
    tw = min(tq, ATTN_SUB_TILE)
    s_len = k_ref.shape[1]
    ck = ATTN_KEY_CHUNK if s_len % ATTN_KEY_CHUNK == 0 else LANES
    n_chunks = s_len // ck
    n_pairs = q_ref.shape[2] // (2 * LANES)
    streams = [(e, r) for r in range(tq // tw) for e in range(2)]
    qs = {}

    def prep_queries(pr):
        for e, r in streams:
            rs = slice(r * tw, (r + 1) * tw)
            c0 = (2 * pr + e) * LANES
            qh = q_ref[0, rs, c0:c0 + LANES].astype(F32)
            if mode == "A":
                ms = jnp.sum(qh * qh, axis=-1, keepdims=True) * (1.0 / HEAD_DIM)
                qh = qh * lax.rsqrt(ms + NORM_EPS) * qnw
            if rope:
                qh = _rope(qh, cos[rs], s1[rs], s2[rs], 16 if mode == "A" else 8)
            qs[pr, e, r] = (qh * (scale * LOG2_E)).astype(BF16)

    m, acc, st = {}, {}, {}

    def scores(c, t):
        pr, j = divmod(t, n_chunks)
        e = c[0]
        h = 2 * pr + e
        kh = k_ref[0, j * ck:(j + 1) * ck, :] if mode == "A" else k_ref[0, j * ck:(j + 1) * ck, h * LANES:(h + 1) * LANES]
        st[c, t] = _dot_nt(kh, qs[(pr,) + c])

    def consume(c, t):
        pr, j = divmod(t, n_chunks)
        e, r = c
        if j == 0:
            m[c] = jnp.full((1, tw), -1e30, F32)
            acc[c] = jnp.zeros((VT_ROWS, tw), F32)
        s = st.pop((c, t))
        m_new = jnp.maximum(m[c], jnp.max(s, axis=0, keepdims=True))
        p = jnp.exp2(s - m_new).astype(BF16)
        vt = vt_ref[0, 0 if mode == "A" else 2 * pr + e, :, j * ck:(j + 1) * ck]
        acc[c] = acc[c] * jnp.exp2(m[c] - m_new) + _dot(vt, p)
        m[c] = m_new
        if j == n_chunks - 1 and e == 1:
            rs = slice(r * tw, (r + 1) * tw)
            o_t = jnp.concatenate([acc[ee, r][0:HEAD_DIM] / acc[ee, r][HEAD_DIM:HEAD_DIM + 1] for ee in range(2)],
                                  axis=0)
            gate = _silu(z_ref[0, rs, pr * LANES:(pr + 1) * LANES].astype(F32))
            o_ref[0, rs, pr * LANES:(pr + 1) * LANES] = (jnp.transpose(o_t) * gate).astype(o_ref.dtype)

    total = n_pairs * n_chunks
    prep_queries(0)
    for t in range(total + ATTN_SKEW):
        if t < total:
            pr, j = divmod(t, n_chunks)
            if j == n_chunks // 2 and pr + 1 < n_pairs:
                prep_queries(pr + 1)
            for c in streams:
                scores(c, t)
        if t >= ATTN_SKEW:
            for c in streams:
                consume(c, t - ATTN_SKEW)


def _attn(ub, k, v, tabs, qnw, *, mode, q_off, z_off):
    b, tq_all = ub.shape[0], ub.shape[1]
    s_len = k.shape[1]
    tq = _tile(tq_all, ATTN_Q_TILE)
    rope = tabs is not None
    npp = ATTN_PAIRS_PER_STEP
    assert N_PAIRS % npp == 0 and A_GROUPS % (2 * npp) == 0
    qw, zw = 2 * npp * LANES, npp * LANES
    qb0, zb0 = q_off // qw, z_off // zw
    in_specs = [pl.BlockSpec((1, tq, qw), lambda bb, p, i: (bb, i, qb0 + p)),
                pl.BlockSpec((1, tq, zw), lambda bb, p, i: (bb, i, zb0 + p))]
    if mode == "A":
        kv_of = A_GROUPS // (2 * npp)
        in_specs += [pl.BlockSpec((1, s_len, LANES), lambda bb, p, i: (bb, 0, 0)),
                     pl.BlockSpec((1, 1, VT_ROWS, s_len), lambda bb, p, i: (bb, p // kv_of, 0, 0))]
        scale = HEAD_DIM ** -0.5
    else:
        in_specs += [pl.BlockSpec((1, s_len, qw), lambda bb, p, i: (bb, 0, p)),
                     pl.BlockSpec((1, 2 * npp, VT_ROWS, s_len), lambda bb, p, i: (bb, p, 0, 0))]
        scale = (B_NOPE + B_ROPE) ** -0.5
    args = [ub, ub, k, v]
    if rope:
        in_specs += [pl.BlockSpec((tq, LANES), lambda bb, p, i: (i, 0))] * 3
        args += list(tabs)
    if mode == "A":
        in_specs += [pl.BlockSpec((1, LANES), lambda bb, p, i: (0, 0))]
        args += [qnw]
    return pl.pallas_call(
        functools.partial(_attn_kernel, mode=mode, rope=rope, scale=scale),
        grid=(b, N_PAIRS // npp, tq_all // tq),
        in_specs=in_specs,
        out_specs=pl.BlockSpec((1, tq, zw), lambda bb, p, i: (bb, i, p)),
        out_shape=jax.ShapeDtypeStruct((b, tq_all, C_WIDTH), BF16),
        compiler_params=_cparams(("parallel", "parallel", "parallel")),
        name="attn_" + mode,
    )(*args)


def _rwkv_prep_kernel(cin_ref, hp_ref, hn_ref, mup_ref, mun_ref, w0_ref, a0_ref, wup_ref, aup_ref,
                      kk_ref, ka_ref, rk_ref, ones_ref,
                      r_o, v_o, kk_o, bonus_o, lwf_o, kf_o, bf_o, lwb_o, kb_o, bb_o):
    i = pl.program_id(1)
    n = pl.num_programs(1)
    s = cin_ref[0]
    tm = s.shape[0]
    prev_row = jnp.where(i > 0, hp_ref[0, SUBLANES - 1:SUBLANES, :], 0.0)
    next_row = jnp.where(i < n - 1, hn_ref[0, 0:1, :], 0.0)
    rows = lax.broadcasted_iota(jnp.int32, (tm, 1), 0)
    prev = jnp.where(rows == 0, prev_row, pltpu.roll(s, 1, 0))
    nxt = jnp.where(rows == tm - 1, next_row, pltpu.roll(s, tm - 1, 0))
    x = s + mup_ref[...] * (prev - s) + mun_ref[...] * (nxt - s)

    w = C_WIDTH
    r = x[:, 0:w]
    k = x[:, w:2 * w]
    v = x[:, 2 * w:3 * w]
    wd = jnp.tanh(x[:, 3 * w:3 * w + LANES]).astype(BF16)
    ad = x[:, 3 * w + LANES:3 * w + 2 * LANES].astype(BF16)
    w_raw = w0_ref[...] + _dot(wd, wup_ref[...])
    logw = (-math.exp(-0.5)) * jax.nn.sigmoid(w_raw)
    a = jax.nn.sigmoid(a0_ref[...] + _dot(ad, aup_ref[...]))
    ones_bd = ones_ref[...]
    kk = k * kk_ref[...]
    kk = kk / jnp.maximum(jnp.sqrt(_head_sum(kk * kk, ones_bd)), 1e-12)
    ka = ka_ref[...]
    a_f, a_b = a[:, 0:w], a[:, w:2 * w]
    k_f = k * (1.0 + (a_f - 1.0) * ka)
    k_b = k * (1.0 + (a_b - 1.0) * ka)
    r_o[0] = r
    v_o[0] = v
    kk_o[0] = kk
    bonus_o[0] = _head_sum(r * (k_f + k_b) * rk_ref[...], ones_bd) * v
    lwf_o[0] = logw[:, 0:w]
    kf_o[0] = k_f
    bf_o[0] = kk * a_f
    lwb_o[0] = logw[:, w:2 * w]
    kb_o[0] = k_b
    bb_o[0] = kk * a_b


def _rwkv_prep(uf, mup, mun, w0, a0, wup, aup, k_k, k_a, r_k, ones_bd):
    b, t = uf.shape[0], uf.shape[1]
    tm = _tile(t, ROW_TILE)
    nblk8 = t // SUBLANES
    step8 = tm // SUBLANES
    const = lambda bb, i: (0, 0)
    in_specs = [
        pl.BlockSpec((1, tm, CIN_PAD), lambda bb, i: (bb, i, 0)),
        pl.BlockSpec((1, SUBLANES, CIN_PAD), lambda bb, i: (bb, jnp.maximum(i * step8 - 1, 0), 0)),
        pl.BlockSpec((1, SUBLANES, CIN_PAD), lambda bb, i: (bb, jnp.minimum((i + 1) * step8, nblk8 - 1), 0)),
        pl.BlockSpec((1, CIN_PAD), const), pl.BlockSpec((1, CIN_PAD), const),
        pl.BlockSpec((1, 2 * C_WIDTH), const), pl.BlockSpec((1, 2 * C_WIDTH), const),
        pl.BlockSpec((LANES, 2 * C_WIDTH), const), pl.BlockSpec((LANES, 2 * C_WIDTH), const),
        pl.BlockSpec((1, C_WIDTH), const), pl.BlockSpec((1, C_WIDTH), const), pl.BlockSpec((1, C_WIDTH), const),
        pl.BlockSpec((2 * LANES, 2 * LANES), const),
    ]
    out_specs = [pl.BlockSpec((1, tm, C_WIDTH), lambda bb, i: (bb, i, 0))] * 10
    out_shape = [jax.ShapeDtypeStruct((b, t, C_WIDTH), F32)] * 10
    return pl.pallas_call(
        _rwkv_prep_kernel,
        grid=(b, t // tm),
        in_specs=in_specs, out_specs=out_specs, out_shape=out_shape,
        compiler_params=_cparams(("parallel", "parallel")),
        name="rwkv_prep",
    )(uf, uf, uf, mup, mun, w0, a0, wup, aup, k_k, k_a, r_k, ones_bd)


def _chunk_operands(r, v, kk, logw, kd, bd, rev):
    c = CHUNK
    rows = lax.broadcasted_iota(jnp.int32, (c, 1), 0)
    cum = logw
    shift = 1
    while shift < c:
        if rev:
            cum = cum + jnp.where(rows < c - shift, pltpu.roll(cum, c - shift, 0), 0.0)
        else:
            cum = cum + jnp.where(rows >= shift, pltpu.roll(cum, shift, 0), 0.0)
        shift *= 2
    tot = cum[0:1, :] if rev else cum[c - 1:c, :]
    e_neg = jnp.exp(-cum)
    e_rem = jnp.exp(tot - cum)
    return dict(a=-kk * jnp.exp(cum - logw), r=r * jnp.exp(cum), b=bd * e_neg, k=kd * e_neg,
                bc=bd * e_rem, kc=kd * e_rem, v=v, g_tot=jnp.exp(tot))


def _scan_chunk(dirs, h_scr, y_refs):
    c = CHUNK
    lane = lax.broadcasted_iota(jnp.int32, (1, LANES), 1)
    m_lo = jnp.where(lane < HEAD_DIM, 1.0, 0.0)
    m_hi = 1.0 - m_lo
    row = lax.broadcasted_iota(jnp.int32, (LANES, LANES), 0)
    col = lax.broadcasted_iota(jnp.int32, (LANES, LANES), 1)
    same = (row // c) == (col // c)
    diag = row == col
    eye = jnp.where(diag, 1.0, 0.0)

    chains = []
    for d_idx, (refs, rev) in enumerate(dirs):
        before = (col > row) if rev else (col < row)
        strict = same & before
        incl = same & (before | diag)
        for p in range(N_PAIRS):
            sl = slice(p * LANES, (p + 1) * LANES)
            ops = _chunk_operands(*(ref[0, :, sl] for ref in refs), rev=rev)
            ch = {name: jnp.concatenate([ops[name] * m_lo, ops[name] * m_hi], axis=0)
                  for name in ("a", "r", "b", "k", "bc", "kc", "v")}
            ch.update(d=d_idx, sl=sl, rev=rev, strict=strict, incl=incl, g_tot=ops["g_tot"])
            ar = jnp.concatenate([ch["a"], ch["r"]], axis=0).astype(BF16)
            bk = jnp.concatenate([ch["b"], ch["k"]], axis=0).astype(BF16)
            ch["sc"] = _dot_nt(ar, bk)
            chains.append(ch)

    for ch in chains:
        sc = ch.pop("sc")
        ch["a_ab"] = jnp.where(ch["strict"], sc[0:LANES, 0:LANES], 0.0)
        a_ak = jnp.where(ch["strict"], sc[0:LANES, LANES:2 * LANES], 0.0)
        ch["m_rb"] = jnp.where(ch["incl"], sc[LANES:2 * LANES, 0:LANES], 0.0).astype(BF16)
        m_rk = jnp.where(ch["incl"], sc[LANES:2 * LANES, LANES:2 * LANES], 0.0)
        ch["vb"] = ch["v"].astype(BF16)
        xm = _dot(jnp.concatenate([a_ak, m_rk], axis=0).astype(BF16), ch["vb"])
        ch["x"] = xm[0:LANES]
        ch["mv"] = xm[LANES:2 * LANES]
        ch["kv"] = _dot_tn(ch["kc"].astype(BF16), ch["vb"])
    s_blk = 1
    while s_blk < c:
        for ch in chains:
            half = (row // s_blk) % 2
            if ch["rev"]:
                off = (half == 0) & ((col // s_blk) == (row // s_blk) + 1)
            else:
                off = (half == 1) & ((col // s_blk) == (row // s_blk) - 1)
            a_off = jnp.where(off, ch["a_ab"], 0.0)
            if s_blk == 1:
                ch["t"] = eye + a_off
            else:
                ch["x_off"] = _dot(a_off.astype(BF16), ch["t"].astype(BF16))
        if s_blk > 1:
            for ch in chains:
                ch["t"] = ch["t"] + _dot(ch["t"].astype(BF16), ch.pop("x_off").astype(BF16))
        s_blk *= 2
    for ch in chains:
        ax = jnp.concatenate([ch["a"], ch["x"]], axis=1).astype(BF16)
        ch["g"] = _dot(ch["t"].astype(BF16), ax).astype(BF16)
    for ch in chains:
        ch["ry"] = jnp.concatenate([ch["r"], ch["mv"]], axis=1) + _dot(ch["m_rb"], ch["g"])
        ch["pq"] = _dot_tn(ch["bc"].astype(BF16), ch["g"])
    for ch in chains:
        q_mat = ch["pq"][:, LANES:2 * LANES] + ch["kv"]
        d_idx, p = ch["d"], ch["sl"].start // LANES
        h_old = h_scr[d_idx, p]
        rp = jnp.concatenate([ch["ry"][:, 0:LANES], ch["pq"][:, 0:LANES]], axis=0).astype(BF16)
        yh = _dot(rp, h_old.astype(BF16))
        y2 = ch["ry"][:, LANES:2 * LANES] + yh[0:LANES]
        g_rows = jnp.transpose(jnp.broadcast_to(ch["g_tot"], (LANES, LANES)))
        h_scr[d_idx, p] = g_rows * h_old + yh[LANES:2 * LANES] + q_mat
        y_refs[d_idx][0, :, ch["sl"]] = y2[0:c, :] + y2[c:2 * c, :]


def _rwkv_scan_kernel(rf, vf, kkf, lwf, kf, bf, rb, vb, kkb, lwb, kb, bb, h0f, h0b,
                      yf, yb, hff, hfb, h_scr):
    i = pl.program_id(1)

    @pl.when(i == 0)
    def _():
        h_scr[0] = h0f[0]
        h_scr[1] = h0b[0]

    _scan_chunk((((rf, vf, kkf, lwf, kf, bf), False), ((rb, vb, kkb, lwb, kb, bb), True)), h_scr, (yf, yb))

    @pl.when(i == pl.num_programs(1) - 1)
    def _():
        hff[0] = h_scr[0]
        hfb[0] = h_scr[1]


def _rwkv_scan(prep, h0f, h0b):
    r, v, kk, _, lwf, kf, bf, lwb, kb, bb = prep
    b, t = r.shape[0], r.shape[1]
    nc = t // CHUNK
    fwd = pl.BlockSpec((1, CHUNK, C_WIDTH), lambda bb_, i: (bb_, i, 0))
    bwd = pl.BlockSpec((1, CHUNK, C_WIDTH), lambda bb_, i: (bb_, nc - 1 - i, 0))
    st = pl.BlockSpec((1, N_PAIRS, LANES, LANES), lambda bb_, i: (bb_, 0, 0, 0))
    return pl.pallas_call(
        _rwkv_scan_kernel,
        grid=(b, nc),
        in_specs=[fwd] * 6 + [bwd] * 6 + [st, st],
        out_specs=[fwd, bwd, st, st],
        out_shape=[jax.ShapeDtypeStruct((b, t, C_WIDTH), F32)] * 2
        + [jax.ShapeDtypeStruct((b, N_PAIRS, LANES, LANES), F32)] * 2,
        scratch_shapes=[pltpu.VMEM((2, N_PAIRS, LANES, LANES), F32)],
        compiler_params=_cparams(("parallel", "arbitrary")),
        name="rwkv_scan",
    )(r, v, kk, lwf, kf, bf, r, v, kk, lwb, kb, bb, h0f, h0b)


def _state_to_pairs(s):
    h = jnp.swapaxes(s.astype(F32), -1, -2).reshape(s.shape[0], N_PAIRS, 2, HEAD_DIM, HEAD_DIM)
    z = jnp.zeros_like(h[:, :, 0])
    top = jnp.concatenate([h[:, :, 0], z], axis=-1)
    bot = jnp.concatenate([z, h[:, :, 1]], axis=-1)
    return jnp.concatenate([top, bot], axis=-2)


def _pairs_to_state(hp):
    e = hp[:, :, 0:HEAD_DIM, 0:HEAD_DIM]
    o = hp[:, :, HEAD_DIM:, HEAD_DIM:]
    h = jnp.stack([e, o], axis=2).reshape(hp.shape[0], N_HEADS, HEAD_DIM, HEAD_DIM)
    return jnp.swapaxes(h, -1, -2)


def _merge_kernel(x_ref, gate_ref, ya_ref, yb_ref, yf_ref, ybw_ref, bonus_ref, cz_ref, g_ref,
                  woa_ref, wob_ref, woc_ref, wout_ref, lnw_ref, lnb_ref, ones_ref, o_ref):
    ones_bd = ones_ref[...]
    y = yf_ref[0] + ybw_ref[0]
    mu = _head_sum(y, ones_bd) * (1.0 / HEAD_DIM)
    yc = y - mu
    var = _head_sum(yc * yc, ones_bd) * (1.0 / HEAD_DIM)
    yn = yc * lax.rsqrt(var + C_GN_EPS) * lnw_ref[...] + lnb_ref[...] + bonus_ref[0]
    ycg = (yn * _silu(cz_ref[0].astype(F32))).astype(BF16)
    d = D_MODEL
    ga = jax.nn.sigmoid(g_ref[0, :, 0:d].astype(F32))
    gb = jax.nn.sigmoid(g_ref[0, :, d:2 * d].astype(F32))
    gc = jax.nn.sigmoid(g_ref[0, :, 2 * d:3 * d].astype(F32))
    mixed = (ga * _dot(ya_ref[0], woa_ref[...]) + gb * _dot(yb_ref[0], wob_ref[...])
             + gc * _dot(ycg, woc_ref[...]))
    out = _dot(mixed.astype(BF16), wout_ref[...])
    o_ref[0] = x_ref[0] + gate_ref[0] * out


def _merge(x, gate, ya, yb, yf, ybw, bonus, ub, woa, wob, woc, wout, lnw, lnb, ones_bd):
    nb, t, d = x.shape
    tm = _tile(t, ROW_TILE)
    tok = lambda w: pl.BlockSpec((1, tm, w), lambda b, i: (b, i, 0))
    const = lambda b, i: (0, 0)
    in_specs = [tok(d), pl.BlockSpec((1, 1, d), lambda b, i: (b, 0, 0)),
                tok(C_WIDTH), tok(C_WIDTH), tok(C_WIDTH), tok(C_WIDTH), tok(C_WIDTH),
                pl.BlockSpec((1, tm, C_WIDTH), lambda b, i: (b, i, OFF_CZ // C_WIDTH)),
                pl.BlockSpec((1, tm, G_W), lambda b, i: (b, i, OFF_G // G_W)),
                pl.BlockSpec((C_WIDTH, d), const), pl.BlockSpec((C_WIDTH, d), const),
                pl.BlockSpec((C_WIDTH, d), const), pl.BlockSpec((d, d), const),
                pl.BlockSpec((1, C_WIDTH), const), pl.BlockSpec((1, C_WIDTH), const),
                pl.BlockSpec((2 * LANES, 2 * LANES), const)]
    return pl.pallas_call(
        _merge_kernel,
        grid=(nb, t // tm),
        in_specs=in_specs,
        out_specs=tok(d),
        out_shape=jax.ShapeDtypeStruct((nb, t, d), F32),
        compiler_params=_cparams(("parallel", "parallel")),
        name="merge",
    )(x, gate, ya, yb, yf, ybw, bonus, ub, ub, woa, wob, woc, wout, lnw, lnb, ones_bd)


def _final_norm_kernel(x_ref, w_ref, o_ref):
    x = x_ref[...]
    o_ref[...] = x * lax.rsqrt(jnp.mean(x * x, axis=-1, keepdims=True) + NORM_EPS) * w_ref[...]


def _final_norm(x, w):
    shape = x.shape
    x2 = x.reshape(-1, shape[-1])
    n, d = x2.shape
    tm = _tile(n, 1024)
    out = pl.pallas_call(
        _final_norm_kernel,
        grid=(n // tm,),
        in_specs=[pl.BlockSpec((tm, d), lambda i: (i, 0)), pl.BlockSpec((1, d), lambda i: (0, 0))],
        out_specs=pl.BlockSpec((tm, d), lambda i: (i, 0)),
        out_shape=jax.ShapeDtypeStruct((n, d), F32),
        compiler_params=_cparams(("parallel",)),
        name="final_norm",
    )(x2, w.reshape(1, d))
    return out.reshape(shape)


def _rope_tables(n_tokens):
    t = np.arange(n_tokens)
    pos = np.stack([t // GRID_W, t % GRID_W], axis=0).astype(np.float32)
    lane = np.arange(LANES)

    def build(active, part, freq_idx, half, first):
        inv = jnp.asarray(ROPE_THETA, F32) ** (-jnp.asarray(freq_idx, F32) / half)
        posm = jnp.asarray(pos)[jnp.asarray(part)]
        ang = posm.T * inv[None, :]
        cos = jnp.where(jnp.asarray(active)[None, :], jnp.cos(ang), 1.0)
        sin = jnp.where(jnp.asarray(active)[None, :], jnp.sin(ang), 0.0)
        s1 = jnp.where(jnp.asarray(first)[None, :], -sin, 0.0)
        s2 = jnp.where(jnp.asarray(first)[None, :], 0.0, sin)
        return cos.astype(F32), s1.astype(F32), s2.astype(F32)

    i = lane % HEAD_DIM
    tab_a = build(np.ones(LANES, bool), i // 32, (i % 32) % 16, 16, (i % 32) < 16)
    i = np.clip(lane - B_NOPE, 0, B_ROPE - 1)
    active = (lane >= B_NOPE) & (lane < B_NOPE + B_ROPE)
    tab_b = build(active, i // 16, (i % 16) % 8, 8, (i % 16) < 8)
    return tab_a, tab_b


def kernel(x_prompt, x_sample, cache_a_k, cache_a_v, cache_b_ckv, cache_b_krope, state_c_fwd, state_c_bwd,
           c, c_ctx, norm_w, w_mod, b_mod, w_in, a_qnorm_w, a_knorm_w, b_kvnorm_w, b_w_uk, b_w_uv,
           c_mu_prev, c_mu_next, c_w0, c_w_up, c_a0, c_a_up, c_k_k, c_k_a, c_r_k, c_lnx_w, c_lnx_b,
           w_oa, w_ob, w_oc, w_out, final_norm_w):
    depth = w_in.shape[0]
    bc, tc, d = x_prompt.shape
    bl, tl, _ = x_sample.shape
    past = cache_a_k.shape[2]
    assert d == D_MODEL and tc % CHUNK == 0 and tl % CHUNK == 0 and tl % GRID_W == 0

    f_idx, b_idx = _column_maps()
    w_all = _gather_cols(w_in, np.concatenate([f_idx, b_idx])).astype(BF16)
    uk_idx = np.full((WIDE,), -1, np.int64)
    for h in range(N_HEADS):
        uk_idx[h * LANES:h * LANES + B_NOPE] = h * B_NOPE + np.arange(B_NOPE)
    wuk = _gather_cols(b_w_uk, uk_idx).astype(BF16)
    wuv = b_w_uv.astype(BF16)
    knw = jnp.tile(a_knorm_w, (1, 2)).reshape(depth, 1, LANES)
    qnw = jnp.tile(a_qnorm_w, (1, 2)).reshape(depth, 1, LANES)
    cnw = b_kvnorm_w.reshape(depth, 1, LANES)
    pad_c = CIN_PAD - C_SHIFT_DIM
    mup = jnp.pad(c_mu_prev, ((0, 0), (0, pad_c))).reshape(depth, 1, CIN_PAD)
    mun = jnp.pad(c_mu_next, ((0, 0), (0, pad_c))).reshape(depth, 1, CIN_PAD)
    w0 = c_w0.reshape(depth, 1, 2 * C_WIDTH)
    a0 = c_a0.reshape(depth, 1, 2 * C_WIDTH)

    def lora_stack(up):
        z = jnp.zeros_like(up[:, 0])
        top = jnp.concatenate([up[:, 0], z], axis=-1)
        bot = jnp.concatenate([z, up[:, 1]], axis=-1)
        return jnp.concatenate([top, bot], axis=-2).astype(BF16)

    wup, aup = lora_stack(c_w_up), lora_stack(c_a_up)
    k_k = c_k_k.reshape(depth, 1, C_WIDTH)
    k_a = c_k_a.reshape(depth, 1, C_WIDTH)
    r_k = c_r_k.reshape(depth, 1, C_WIDTH)
    lnw = c_lnx_w.reshape(depth, 1, C_WIDTH)
    lnb = c_lnx_b.reshape(depth, 1, C_WIDTH)
    woa, wob, woc, wout = (w.astype(BF16) for w in (w_oa, w_ob, w_oc, w_out))
    hid = np.arange(2 * LANES) // HEAD_DIM
    ones_bd = jnp.asarray((hid[:, None] == hid[None, :]).astype(np.float32)).astype(BF16)
    tab_a, tab_b = _rope_tables(tl)

    rows = -(-(bl + 1) // SUBLANES) * SUBLANES
    cond = jnp.zeros((rows, d), F32).at[0:bl].set(c).at[bl].set(c_ctx)
    mod = _modulation(cond, w_mod, b_mod)

    def mod_parts(l, lo, hi):
        m = mod[l, lo:hi]
        return tuple(m[:, j * d:(j + 1) * d].reshape(hi - lo, 1, d) for j in range(3))

    zeros_state = jnp.zeros((bc, N_PAIRS, LANES, LANES), F32)
    cache_kr_pad = jnp.pad(cache_b_krope, ((0, 0), (0, 0), (0, 0), (B_NOPE, LANES - B_NOPE - B_ROPE)))
    cache_kv = jnp.concatenate([cache_a_k.reshape(bl, depth, past, LANES),
                                cache_a_v.reshape(bl, depth, past, LANES),
                                cache_b_ckv, cache_kr_pad], axis=-1)

    xp = x_prompt.reshape(1, bc * tc, d)
    xs = x_sample
    new_ak, new_av, new_ckv, new_kr, new_sf, new_sb = [], [], [], [], [], []

    def mixer_tail(x, gate, ub, uf, ka, va, kb, vb, tabs_q, h0f, h0b, l, nb_tok):
        b_, t_ = ub.shape[0], ub.shape[1]
        ya = _attn(ub, ka, va, tabs_q[0], qnw[l], mode="A", q_off=OFF_AQ, z_off=OFF_AZ)
        yb = _attn(ub, kb, vb, tabs_q[1], None, mode="B", q_off=OFF_BQ, z_off=OFF_BZ)
        prep = _rwkv_prep(uf, mup[l], mun[l], w0[l], a0[l], wup[l], aup[l], k_k[l], k_a[l], r_k[l], ones_bd)
        yf, ybw, hff, hfb = _rwkv_scan(prep, h0f, h0b)
        rs = lambda z: z.reshape(nb_tok, (b_ * t_) // nb_tok, z.shape[-1])
        x_new = _merge(x, gate, rs(ya), rs(yb), rs(yf), rs(ybw), rs(prep[3]), rs(ub),
                       woa[l], wob[l], woc[l], wout[l], lnw[l], lnb[l], ones_bd)
        return x_new, hff, hfb

    for l in range(depth):
        nw = norm_w[l].reshape(1, d)
        shift, scale, gate = mod_parts(l, bl, bl + 1)
        uf, ub = _proj(xp, shift, scale, nw, w_all[l])
        uf, ub = uf.reshape(bc, tc, NF), ub.reshape(bc, tc, NB16)
        ka, va, kb, vb, k_n, ckv_n = _kv_prep(uf, CIN_PAD // KV_W, None, None, knw[l], cnw[l], wuk[l], wuv[l],
                                               emit=True)
        xp, hff, hfb = mixer_tail(xp, gate, ub, uf, ka, va, kb, vb, (None, None), zeros_state, zeros_state, l, 1)
        new_ak.append(k_n.reshape(bc, tc, A_KV_HEADS, HEAD_DIM))
        new_av.append(uf[:, :, CIN_PAD + LANES:CIN_PAD + 2 * LANES].reshape(bc, tc, A_KV_HEADS, HEAD_DIM))
        new_ckv.append(ckv_n)
        new_kr.append(uf[:, :, CIN_PAD + 3 * LANES + B_NOPE:CIN_PAD + 3 * LANES + B_NOPE + B_ROPE])
        new_sf.append(_pairs_to_state(hff))
        new_sb.append(_pairs_to_state(hfb))
        shift, scale, gate = mod_parts(l, 0, bl)
        uf, ub = _proj(xs, shift, scale, nw, w_all[l])
        ka, va, kb, vb = _kv_prep(uf, CIN_PAD // KV_W, cache_kv[:, l], tab_a + tab_b,
                                  knw[l], cnw[l], wuk[l], wuv[l], emit=False)
        xs, _, _ = mixer_tail(xs, gate, ub, uf, ka, va, kb, vb, (tab_a, tab_b),
                              _state_to_pairs(state_c_fwd[:, l]), _state_to_pairs(state_c_bwd[:, l]), l, bl)

    y_prompt = _final_norm(xp, final_norm_w).reshape(bc, tc, d)
    y_sample = _final_norm(xs, final_norm_w)
    return (y_prompt, y_sample,
            jnp.stack(new_ak, axis=1), jnp.stack(new_av, axis=1),
            jnp.stack(new_ckv, axis=1), jnp.stack(new_kr, axis=1),
            jnp.stack(new_sf, axis=1), jnp.stack(new_sb, axis=1))
```
